```python
import jax, jax.numpy as jnp
from jax import lax
import numpy as np

D_MODEL = 4096
BATCH = 32
SEQ = 256
DEPTH = 2
DEC_BATCH = 2
DEC_SEQ = 1024
PAST_LEN = 256

GRID_W = 64
RET_HEADS = 8
RET_DK = 128
RET_DV = 128
RET_CHUNK = 128
RET_W = RET_HEADS * RET_DK
RET_VW = RET_HEADS * RET_DV
NA_HEADS = 8
NA_DH = 128
NA_ROWS = 8
NA_COLS = 16
NA_W = NA_HEADS * NA_DH
ATTN_BLOCK = 128
RW_HEADS = 16
RW_N = 64
RW_W = RW_HEADS * RW_N
RW_DECAY_LORA = 64
RW_AAA_LORA = 64
RW_GATE_LORA = 128
IN_SIZES = (RET_W, RET_W, RET_VW, RET_VW, NA_W, NA_W, NA_W, 3 * RW_W)
IN_WIDTH = sum(IN_SIZES)
N_BRANCH = 3
N_EXPERTS = 16
N_GROUPS = 4
EXPERTS_PER_GROUP = N_EXPERTS // N_GROUPS
TOP_K = 2
EXPERT_FF = D_MODEL // 4
ROPE_BASE = 10000.0
NORM_EPS = 1e-6
NEG_INF = -1e30

kernel_name = "hybrid_diffusion_retention_natten_rwkv7_moe_step"

F32 = jnp.float32


def rmsnorm(x, g):
    xf = x.astype(F32)
    y = xf * lax.rsqrt(jnp.mean(xf * xf, -1, keepdims=True) + NORM_EPS)
    return (y * g.astype(F32)).astype(x.dtype)


def group_norm(y, g, eps):
    mu = jnp.mean(y, -1, keepdims=True)
    var = jnp.mean(jnp.square(y - mu), -1, keepdims=True)
    yn = (y - mu) * lax.rsqrt(var + eps)
    return yn.reshape(y.shape[0], y.shape[1], -1) * g.astype(F32)


def rope_2d(x):
    T, dh = x.shape[2], x.shape[3]
    nf = dh // 4
    t = jnp.arange(T)
    pos = jnp.stack([t // GRID_W, t % GRID_W], -1).astype(F32)
    inv = ROPE_BASE ** (-jnp.arange(nf, dtype=F32) / nf)
    ang = pos[:, :, None] * inv
    cos, sin = jnp.cos(ang), jnp.sin(ang)
    xr = x.reshape(x.shape[0], x.shape[1], T, 2, 2, nf)
    x1, x2 = xr[..., 0, :], xr[..., 1, :]
    return jnp.stack([x1 * cos - x2 * sin, x2 * cos + x1 * sin], -2).reshape(x.shape)


def retention_chunkwise(q, k, v, log_gamma, s0):
    B, H, T, dk = q.shape
    dv = v.shape[-1]
    C = RET_CHUNK
    n = T // C
    qc = q.reshape(B, H, n, C, dk)
    kc = k.reshape(B, H, n, C, dk)
    vc = v.reshape(B, H, n, C, dv)
    pos = jnp.arange(C, dtype=F32)
    lg = log_gamma.astype(F32)[:, None]
    diff = pos[:, None] - pos[None, :]
    dmask = jnp.where(diff >= 0, jnp.exp(lg[:, :, None] * jnp.maximum(diff, 0.0)), 0.0)
    scores = jnp.einsum('bhnid,bhnjd->bhnij', qc, kc) * dmask[None, :, None]
    inner = jnp.einsum('bhnij,bhnje->bhnie', scores, vc)
    q_dec = jnp.exp(lg * (pos + 1.0))
    k_dec = jnp.exp(lg * (C - 1.0 - pos))
    chunk_kv = jnp.einsum('bhnjd,bhnje->nbhde', kc * k_dec[None, :, None, :, None], vc)
    chunk_decay = jnp.exp(lg * C)[None, :, :, None]

    def step(S, kv):
        return chunk_decay * S + kv, S

    s_final, s_prev = lax.scan(step, s0.astype(F32), chunk_kv)
    cross = jnp.einsum('bhnid,nbhde->bhnie', qc * q_dec[None, :, None, :, None], s_prev)
    return (inner + cross).reshape(B, H, T, dv), s_final


def retention_branch(q, k, v, g, p, latent, s0):
    B, T, _ = q.shape

    def heads(a, d):
        return a.astype(F32).reshape(B, T, RET_HEADS, d).transpose(0, 2, 1, 3)

    q, k, v = heads(q, RET_DK), heads(k, RET_DK), heads(v, RET_DV)
    if latent:
        q, k = rope_2d(q), rope_2d(k)
    k = k * (RET_DK ** -0.5)
    yf, sf = retention_chunkwise(q, k, v, p['ret_log_decay'][0], s0[0])
    yb, sb = retention_chunkwise(q[:, :, ::-1], k[:, :, ::-1], v[:, :, ::-1], p['ret_log_decay'][1], s0[1])
    y = (yf + yb[:, :, ::-1]).transpose(0, 2, 1, 3)
    y = group_norm(y, p['ret_gn'], 1e-5) * jax.nn.silu(g.astype(F32))
    return y, (sf, sb)


def dense_attn(q, k, v):
    B, H, L, dh = q.shape
    nb = L // ATTN_BLOCK
    qb = jnp.moveaxis(q.reshape(B, H, nb, ATTN_BLOCK, dh), 2, 0)

    def blk(qi):
        pr = jax.nn.softmax(jnp.einsum('bhqd,bhkd->bhqk', qi, k), -1)
        return jnp.einsum('bhqk,bhkd->bhqd', pr, v)

    out = lax.map(blk, qb)
    return jnp.moveaxis(out, 0, 2).reshape(B, H, L, dh)


def neighbourhood_attn(q, k, v, rpb, ctx_k, ctx_v):
    B, H, T, dh = q.shape
    rows = T // GRID_W
    kr = min(NA_ROWS, rows)
    nl = kr * GRID_W
    qg = q.reshape(B, H, rows, GRID_W, dh)
    kg = k.reshape(B, H, rows, GRID_W, dh)
    vg = v.reshape(B, H, rows, GRID_W, dh)
    r_ids = jnp.arange(rows)
    row_start = jnp.clip(r_ids - kr // 2, 0, rows - kr)
    row_idx = row_start[:, None] + jnp.arange(kr)[None, :]
    kw = kg[:, :, row_idx].reshape(B, H, rows, nl, dh)
    vw = vg[:, :, row_idx].reshape(B, H, rows, nl, dh)
    cols = jnp.arange(GRID_W)
    col_start = jnp.clip(cols - NA_COLS // 2, 0, GRID_W - NA_COLS)
    col_ok = (cols[None, :] >= col_start[:, None]) & (cols[None, :] < col_start[:, None] + NA_COLS)
    mask = jnp.broadcast_to(col_ok[:, None, :], (GRID_W, kr, GRID_W)).reshape(GRID_W, nl)
    dr_idx = row_idx - r_ids[:, None] + (NA_ROWS - 1)
    dc_idx = jnp.clip(cols[None, :] - cols[:, None] + (NA_COLS - 1), 0, 2 * NA_COLS - 2)
    bias = rpb[:, dr_idx[:, None, :, None], dc_idx[None, :, None, :]].reshape(H, rows, GRID_W, nl)
    s_loc = jnp.einsum('bhrqd,bhrkd->bhrqk', qg, kw) + bias.astype(F32)[None]
    s_loc = jnp.where(mask, s_loc, NEG_INF)
    s_ctx = jnp.einsum('bhrqd,bhld->bhrql', qg, ctx_k)
    pr = jax.nn.softmax(jnp.concatenate([s_loc, s_ctx], -1), -1)
    out = jnp.einsum('bhrqk,bhrkd->bhrqd', pr[..., :nl], vw) + jnp.einsum('bhrql,bhld->bhrqd', pr[..., nl:], ctx_v)
    return out.reshape(B, H, T, dh)


def na_branch(q, k, v, p, ctx_kv):
    B, T, _ = q.shape

    def heads(a):
        return a.astype(F32).reshape(B, T, NA_HEADS, NA_DH).transpose(0, 2, 1, 3)

    q = rmsnorm(heads(q), p['na_qn']) * (NA_DH ** -0.5)
    k = rmsnorm(heads(k), p['na_kn'])
    v = heads(v)
    if ctx_kv is None:
        y = dense_attn(q, k, v)
        new_kv = (k, v)
    else:
        y = neighbourhood_attn(q, k, v, p['na_rpb'], ctx_kv[0].astype(F32), ctx_kv[1].astype(F32))
        new_kv = None
    return y.transpose(0, 2, 1, 3).reshape(B, T, NA_W), new_kv


def rwkv7_scan(r, w, k, v, kk, a, s0):
    xs = tuple(jnp.moveaxis(t, 1, 0) for t in (r, w, k, v, kk, a))

    def step(S, inp):
        r_t, w_t, k_t, v_t, kk_t, a_t = inp
        sa = jnp.einsum('bhij,bhj->bhi', S, -kk_t)
        S = S * w_t[:, :, None, :] + sa[..., None] * (kk_t * a_t)[:, :, None, :] + v_t[..., None] * k_t[:, :, None, :]
        return S, jnp.einsum('bhij,bhj->bhi', S, r_t)

    s_final, y = lax.scan(step, s0.astype(F32), xs)
    return jnp.moveaxis(y, 0, 1), s_final


def rwkv_branch(h, rkv, p, s0):
    B, T, _ = rkv.shape
    xp = jnp.pad(rkv.astype(F32), ((0, 0), (1, 1), (0, 0)))
    sh = p['rw_shift'].astype(F32)
    rkv = sh[0] * xp[:, :-2] + sh[1] * xp[:, 1:-1] + sh[2] * xp[:, 2:]
    r, k, v = (t.reshape(B, T, RW_HEADS, RW_N) for t in jnp.split(rkv, 3, -1))
    hf = h.astype(F32)
    g = jax.nn.sigmoid(hf @ p['rw_g1'].astype(F32)) @ p['rw_g2'].astype(F32)
    kk = k * p['rw_kk'].astype(F32).reshape(RW_HEADS, RW_N)
    kk = kk / jnp.maximum(jnp.sqrt(jnp.sum(kk * kk, -1, keepdims=True)), 1e-6)
    k_a = p['rw_ka'].astype(F32).reshape(RW_HEADS, RW_N)
    ys, states = [], []
    for d in range(2):
        wlog = -jax.nn.softplus(-(p['rw_w0'][d] + jnp.tanh(hf @ p['rw_w1'][d]) @ p['rw_w2'][d])) - 0.5
        w = jnp.exp(-jnp.exp(wlog.astype(F32))).reshape(B, T, RW_HEADS, RW_N)
        a = jax.nn.sigmoid(p['rw_a0'][d] + (hf @ p['rw_a1'][d]) @ p['rw_a2'][d]).astype(F32).reshape(B, T, RW_HEADS, RW_N)
        kd = k * (1.0 + (a - 1.0) * k_a)
        seqs = (r, w, kd, v, kk, a)
        if d == 1:
            seqs = tuple(t[:, ::-1] for t in seqs)
        y, s = rwkv7_scan(*seqs, s0[d])
        ys.append(y if d == 0 else y[:, ::-1])
        states.append(s)
    y = ys[0] + ys[1]
    bonus = jnp.sum(r * k * p['rw_rk'].astype(F32), -1, keepdims=True) * v
    y = group_norm(y, p['rw_gn'], 64e-5) + bonus.reshape(B, T, RW_W)
    return y * g, (states[0], states[1])


def mixer(h, p, ctx):
    B = h.shape[0]
    latent = ctx is not None
    dt = h.dtype
    proj = h @ p['w_in']
    split_at = [int(s) for s in np.cumsum(IN_SIZES)[:-1]]
    ret_q, ret_k, ret_v, ret_g, na_q, na_k, na_v, rw_rkv = jnp.split(proj, split_at, -1)
    if latent:
        s_ret = (ctx[0], ctx[1])
        kv_na = (ctx[2], ctx[3])
        s_rw = (ctx[4], ctx[5])
    else:
        z_ret = jnp.zeros((B, RET_HEADS, RET_DK, RET_DV), F32)
        z_rw = jnp.zeros((B, RW_HEADS, RW_N, RW_N), F32)
        s_ret, kv_na, s_rw = (z_ret, z_ret), None, (z_rw, z_rw)
    y_ret, st_ret = retention_branch(ret_q, ret_k, ret_v, ret_g, p, latent, s_ret)
    y_na, new_kv = na_branch(na_q, na_k, na_v, p, kv_na)
    y_rw, st_rw = rwkv_branch(h, rw_rkv, p, s_rw)
    gates = jax.nn.sigmoid((h @ p['w_gate'] + p['b_gate']).astype(F32))
    g_ret, g_na, g_rw = jnp.split(gates, N_BRANCH, -1)
    merged = (g_ret * (y_ret.astype(dt) @ p['w_br_ret'])
              + g_na * (y_na.astype(dt) @ p['w_br_na'])
              + g_rw * (y_rw.astype(dt) @ p['w_br_rw']))
    out = merged.astype(dt) @ p['w_out']
    new_ctx = None if latent else (st_ret[0], st_ret[1], new_kv[0], new_kv[1], st_rw[0], st_rw[1])
    return out, new_ctx


def moe(h, p, router_w, router_bias):
    B, T, D = h.shape
    x = h.reshape(B * T, D)
    scores = jax.nn.sigmoid((x @ router_w).astype(F32))
    sel = scores + router_bias.astype(F32)
    grp = sel.reshape(-1, N_GROUPS, EXPERTS_PER_GROUP)
    grp_score = jnp.sum(lax.top_k(grp, 2)[0], -1)
    best = jnp.argmax(grp_score, -1)
    in_grp = jnp.arange(N_GROUPS)[None, :] == best[:, None]
    masked = jnp.where(in_grp[:, :, None], grp, -jnp.inf).reshape(-1, N_EXPERTS)
    _, idx = lax.top_k(masked, TOP_K)
    w = jnp.take_along_axis(scores, idx, -1)
    w = w / jnp.sum(w, -1, keepdims=True)
    combine = jnp.sum(jax.nn.one_hot(idx, N_EXPERTS, dtype=F32) * w[..., None], 1).astype(x.dtype)

    def expert(acc, xs):
        wg, wu, wd, cw = xs
        return acc + cw[:, None] * ((jax.nn.silu(x @ wg) * (x @ wu)) @ wd), None

    y, _ = lax.scan(expert, jnp.zeros_like(x), (p['moe_w_gate'], p['moe_w_up'], p['moe_w_down'], combine.T))
    return y.reshape(B, T, D)


def layer(x, mod, p, router_w, router_bias, ctx):
    sh1, sc1, g1, sh2, sc2, g2 = jnp.split(mod, 6, -1)
    h = rmsnorm(x, p['norm_mix']) * (1.0 + sc1) + sh1
    m, new_ctx = mixer(h, p, ctx)
    x = x + g1 * m
    h = rmsnorm(x, p['norm_ffn']) * (1.0 + sc2) + sh2
    x = x + g2 * moe(h, p, router_w, router_bias)
    return x, new_ctx


def setup_inputs(seed: int = 0) -> dict:
    key = jax.random.key(seed)
    ks = iter(jax.random.split(key, 64))

    def nrm(shape, s):
        return jax.random.normal(next(ks), shape, F32) * s

    D = D_MODEL
    base_decay = jnp.log(1.0 - 2.0 ** (-5.0 - jnp.arange(RET_HEADS, dtype=F32)))
    shift_init = jnp.array([0.25, 1.0, 0.25], F32)[None, :, None]
    return {
        "x_prompt": nrm((BATCH, SEQ, D), 1.0),
        "x_sample": nrm((DEC_BATCH, DEC_SEQ, D), 1.0),
        "state_ret_fwd": nrm((DEC_BATCH, DEPTH, RET_HEADS, RET_DK, RET_DV), 0.1),
        "state_ret_bwd": nrm((DEC_BATCH, DEPTH, RET_HEADS, RET_DK, RET_DV), 0.1),
        "cache_na_k": nrm((DEC_BATCH, DEPTH, NA_HEADS, PAST_LEN, NA_DH), 1.0),
        "cache_na_v": nrm((DEC_BATCH, DEPTH, NA_HEADS, PAST_LEN, NA_DH), 1.0),
        "state_rwkv_fwd": nrm((DEC_BATCH, DEPTH, RW_HEADS, RW_N, RW_N), 0.1),
        "state_rwkv_bwd": nrm((DEC_BATCH, DEPTH, RW_HEADS, RW_N, RW_N), 0.1),
        "c": nrm((DEC_BATCH, D), 1.0),
        "c_ctx": nrm((D,), 1.0),
        "w_mod": nrm((DEPTH, D, 6 * D), 0.5 * D ** -0.5),
        "b_mod": nrm((DEPTH, 6 * D), 0.01),
        "norm_mix": 1.0 + nrm((DEPTH, D), 0.02),
        "norm_ffn": 1.0 + nrm((DEPTH, D), 0.02),
        "w_in": nrm((DEPTH, D, IN_WIDTH), D ** -0.5),
        "w_gate": nrm((DEPTH, D, N_BRANCH * D), D ** -0.5),
        "b_gate": nrm((DEPTH, N_BRANCH * D), 0.01),
        "w_br_ret": nrm((DEPTH, RET_VW, D), RET_VW ** -0.5),
        "w_br_na": nrm((DEPTH, NA_W, D), NA_W ** -0.5),
        "w_br_rw": nrm((DEPTH, RW_W, D), RW_W ** -0.5),
        "w_out": nrm((DEPTH, D, D), D ** -0.5),
        "ret_log_decay": base_decay * (1.0 + nrm((DEPTH, 2, RET_HEADS), 0.05)),
        "ret_gn": 1.0 + nrm((DEPTH, RET_VW), 0.02),
        "na_qn": 1.0 + nrm((DEPTH, NA_DH), 0.02),
        "na_kn": 1.0 + nrm((DEPTH, NA_DH), 0.02),
        "na_rpb": nrm((DEPTH, NA_HEADS, 2 * NA_ROWS - 1, 2 * NA_COLS - 1), 0.1),
        "rw_shift": shift_init + nrm((DEPTH, 3, 3 * RW_W), 0.02),
        "rw_w0": nrm((DEPTH, 2, RW_W), 0.5),
        "rw_w1": nrm((DEPTH, 2, D, RW_DECAY_LORA), D ** -0.5),
        "rw_w2": nrm((DEPTH, 2, RW_DECAY_LORA, RW_W), 0.1 * RW_DECAY_LORA ** -0.5),
        "rw_a0": nrm((DEPTH, 2, RW_W), 0.1),
        "rw_a1": nrm((DEPTH, 2, D, RW_AAA_LORA), D ** -0.5),
        "rw_a2": nrm((DEPTH, 2, RW_AAA_LORA, RW_W), 0.1 * RW_AAA_LORA ** -0.5),
        "rw_g1": nrm((DEPTH, D, RW_GATE_LORA), D ** -0.5),
        "rw_g2": nrm((DEPTH, RW_GATE_LORA, RW_W), RW_GATE_LORA ** -0.5),
        "rw_kk": 0.85 + nrm((DEPTH, RW_W), 0.02),
        "rw_ka": 1.0 + nrm((DEPTH, RW_W), 0.02),
        "rw_rk": nrm((DEPTH, RW_HEADS, RW_N), 0.1),
        "rw_gn": 1.0 + nrm((DEPTH, RW_W), 0.02),
        "router_w": nrm((D, N_EXPERTS), D ** -0.5),
        "router_bias": nrm((N_EXPERTS,), 0.01),
        "moe_w_gate": nrm((DEPTH, N_EXPERTS, D, EXPERT_FF), D ** -0.5),
        "moe_w_up": nrm((DEPTH, N_EXPERTS, D, EXPERT_FF), D ** -0.5),
        "moe_w_down": nrm((DEPTH, N_EXPERTS, EXPERT_FF, D), EXPERT_FF ** -0.5),
    }


def reference(x_prompt, x_sample, state_ret_fwd, state_ret_bwd, cache_na_k, cache_na_v,
              state_rwkv_fwd, state_rwkv_bwd, c, c_ctx, w_mod, b_mod, norm_mix, norm_ffn,
              w_in, w_gate, b_gate, w_br_ret, w_br_na, w_br_rw, w_out, ret_log_decay, ret_gn,
              na_qn, na_kn, na_rpb, rw_shift, rw_w0, rw_w1, rw_w2, rw_a0, rw_a1, rw_a2,
              rw_g1, rw_g2, rw_kk, rw_ka, rw_rk, rw_gn, router_w, router_bias,
              moe_w_gate, moe_w_up, moe_w_down):
    y_p, y_s = x_prompt, x_sample
    ret_f, ret_b, na_k, na_v, rw_f, rw_b = [], [], [], [], [], []
    for l in range(DEPTH):
        p = dict(norm_mix=norm_mix[l], norm_ffn=norm_ffn[l], w_in=w_in[l], w_gate=w_gate[l],
                 b_gate=b_gate[l], w_br_ret=w_br_ret[l], w_br_na=w_br_na[l], w_br_rw=w_br_rw[l],
                 w_out=w_out[l], ret_log_decay=ret_log_decay[l], ret_gn=ret_gn[l], na_qn=na_qn[l],
                 na_kn=na_kn[l], na_rpb=na_rpb[l], rw_shift=rw_shift[l], rw_w0=rw_w0[l],
                 rw_w1=rw_w1[l], rw_w2=rw_w2[l], rw_a0=rw_a0[l], rw_a1=rw_a1[l], rw_a2=rw_a2[l],
                 rw_g1=rw_g1[l], rw_g2=rw_g2[l], rw_kk=rw_kk[l], rw_ka=rw_ka[l], rw_rk=rw_rk[l],
                 rw_gn=rw_gn[l], moe_w_gate=moe_w_gate[l], moe_w_up=moe_w_up[l],
                 moe_w_down=moe_w_down[l])
        mod_ctx = jax.nn.silu(c_ctx) @ w_mod[l] + b_mod[l]
        mod_lat = (jax.nn.silu(c) @ w_mod[l] + b_mod[l])[:, None, :]
        y_p, st = layer(y_p, mod_ctx, p, router_w, router_bias, None)
        ret_f.append(st[0]); ret_b.append(st[1]); na_k.append(st[2])
        na_v.append(st[3]); rw_f.append(st[4]); rw_b.append(st[5])
        ctx = (state_ret_fwd[:, l], state_ret_bwd[:, l], cache_na_k[:, l], cache_na_v[:, l],
               state_rwkv_fwd[:, l], state_rwkv_bwd[:, l])
        y_s, _ = layer(y_s, mod_lat, p, router_w, router_bias, ctx)
    new_ret_fwd = jnp.stack(ret_f, 1)
    new_ret_bwd = jnp.stack(ret_b, 1)
    new_na_k = jnp.stack(na_k, 1)
    new_na_v = jnp.stack(na_v, 1)
    new_rwkv_fwd = jnp.stack(rw_f, 1)
    new_rwkv_bwd = jnp.stack(rw_b, 1)
    return (y_p, y_s, new_ret_fwd, new_ret_bwd, new_na_k, new_na_v, new_rwkv_fwd, new_rwkv_bwd)
```

```python
import functools

import jax
import jax.numpy as jnp
import numpy as np
from jax import lax
from jax.experimental import pallas as pl
from jax.experimental.pallas import tpu as pltpu

F32 = jnp.float32
BF16 = jnp.bfloat16

GRID_W = 64
RET_HEADS = 8
RET_D = 128
NA_HEADS = 8
NA_DH = 128
NA_ROWS = 8
NA_COLS = 16
RW_HEADS = 16
RW_N = 64
RW_W = RW_HEADS * RW_N
N_EXPERTS = 16
N_GROUPS = 4
EXPERTS_PER_GROUP = 4
ROPE_BASE = 10000.0
NORM_EPS = 1e-6
NEG_INF = -1e30

LANES = 128
RW_CHUNK = 64
RW_CB = 256
LORA_W = 640
ROW_TILE = 512
MOE_TILE = 256
VMEM_LIMIT = 56 * 1024 * 1024


def _cparams(sem, vmem=None):
    return pltpu.CompilerParams(dimension_semantics=sem, vmem_limit_bytes=vmem)


def _dot(a, b):
    return jnp.dot(a.astype(BF16), b.astype(BF16), preferred_element_type=F32)


def _dot_nt(a, b):
    return lax.dot_general(a.astype(BF16), b.astype(BF16), (((1,), (1,)), ((), ())),
                           preferred_element_type=F32)


def _dot_tn(a, b):
    return lax.dot_general(a.astype(BF16), b.astype(BF16), (((0,), (0,)), ((), ())),
                           preferred_element_type=F32)


def _split3(x):
    h = x.astype(BF16)
    r1 = x - h.astype(F32)
    m = r1.astype(BF16)
    l = (r1 - m.astype(F32)).astype(BF16)
    return h, m, l


def _dot_exact_lhs(c, x):
    h, m, l = _split3(x)
    cb = c.astype(BF16)
    return (jnp.dot(cb, h, preferred_element_type=F32) + jnp.dot(cb, m, preferred_element_type=F32)
            + jnp.dot(cb, l, preferred_element_type=F32))


def _dot_exact_rhs(x, c):
    h, m, l = _split3(x)
    cb = c.astype(BF16)
    return (jnp.dot(h, cb, preferred_element_type=F32) + jnp.dot(m, cb, preferred_element_type=F32)
            + jnp.dot(l, cb, preferred_element_type=F32))


def _head_sum_matrix():
    r = lax.broadcasted_iota(jnp.int32, (LANES, LANES), 0) // RW_N
    c = lax.broadcasted_iota(jnp.int32, (LANES, LANES), 1) // RW_N
    return (r == c).astype(F32)


def _seg_sum(x, p):
    parts = [_dot_exact_rhs(x[:, s * LANES:(s + 1) * LANES], p) for s in range(x.shape[1] // LANES)]
    return parts[0] if len(parts) == 1 else jnp.concatenate(parts, axis=1)


def _sigmoid(x):
    return 1.0 / (1.0 + jnp.exp(-x))


def _silu(x):
    return x * _sigmoid(x)


def _mod_kernel(c_ref, w_ref, b_ref, o_ref):
    a = _silu(c_ref[...])
    o_ref[...] = _dot(a, w_ref[...]) + b_ref[...]


def _mod_call(c8, w_mod, b_mod3, l):
    D = c8.shape[1]
    n_out = w_mod.shape[2]
    tn = 512
    return pl.pallas_call(
        _mod_kernel,
        out_shape=jax.ShapeDtypeStruct((8, n_out), F32),
        grid=(n_out // tn,),
        in_specs=[pl.BlockSpec((8, D), lambda j: (0, 0)),
                  pl.BlockSpec((None, D, tn), lambda j: (l, 0, j)),
                  pl.BlockSpec((None, 1, tn), lambda j: (l, 0, j))],
        out_specs=pl.BlockSpec((8, tn), lambda j: (0, j)),
        compiler_params=_cparams(("arbitrary",), VMEM_LIMIT),
        name="mod_proj",
    )(c8, w_mod, b_mod3)


def _norm_mod_kernel(x_ref, g_ref, mod_ref, *out_refs, sh_idx, sc_idx):
    x = x_ref[...]
    y = x * lax.rsqrt(jnp.mean(x * x, -1, keepdims=True) + NORM_EPS) * g_ref[...]
    h = y * (1.0 + mod_ref[sc_idx:sc_idx + 1, :]) + mod_ref[sh_idx:sh_idx + 1, :]
    for o in out_refs:
        o[...] = h.astype(o.dtype)


def _mod_row_map(tm, n_prompt, t_s):
    def row(i):
        r = i * tm
        return jnp.where(r < n_prompt, 0, 1 + (r - n_prompt) // t_s)
    return row


def _norm_mod_call(x, g3, mod, l, sh_idx, sc_idx, n_prompt, t_s, out_dtypes):
    N, D = x.shape
    tm = 256
    row = _mod_row_map(tm, n_prompt, t_s)
    outs = tuple(jax.ShapeDtypeStruct((N, D), dt) for dt in out_dtypes)
    return pl.pallas_call(
        functools.partial(_norm_mod_kernel, sh_idx=sh_idx, sc_idx=sc_idx),
        out_shape=outs,
        grid=(N // tm,),
        in_specs=[pl.BlockSpec((tm, D), lambda i: (i, 0)),
                  pl.BlockSpec((None, 1, D), lambda i: (l, 0, 0)),
                  pl.BlockSpec((None, 6, D), lambda i: (row(i), 0, 0))],
        out_specs=tuple(pl.BlockSpec((tm, D), lambda i: (i, 0)) for _ in out_dtypes),
        compiler_params=_cparams(("arbitrary",), VMEM_LIMIT),
        name="norm_mod",
    )(x, g3, mod)


def _mm_kernel(a_ref, w_ref, o_ref, wbf_ref):
    @pl.when(pl.program_id(1) == 0)
    def _():
        wbf_ref[...] = w_ref[...].astype(BF16)

    o_ref[...] = jnp.dot(a_ref[...], wbf_ref[...], preferred_element_type=F32).astype(o_ref.dtype)


def _mm_call(a, w, l, tn, name, out_dtype=F32):
    N, K = a.shape
    M = w.shape[-1]
    tm = ROW_TILE
    if w.ndim == 3:
        w_spec = pl.BlockSpec((None, K, tn), lambda j, i: (l, 0, j))
    else:
        w_spec = pl.BlockSpec((K, tn), lambda j, i: (0, j))
    return pl.pallas_call(
        _mm_kernel,
        out_shape=jax.ShapeDtypeStruct((N, M), out_dtype),
        grid=(M // tn, N // tm),
        in_specs=[pl.BlockSpec((tm, K), lambda j, i: (i, 0)), w_spec],
        out_specs=pl.BlockSpec((tm, tn), lambda j, i: (i, j)),
        scratch_shapes=[pltpu.VMEM((K, tn), BF16)],
        compiler_params=_cparams(("arbitrary", "arbitrary"), VMEM_LIMIT),
        name=name,
    )(a, w)


def _rope(x, cos, sin):
    lane = lax.broadcasted_iota(jnp.int32, x.shape, 1)
    first = (lane % 64) < 32
    rot = jnp.where(first, pltpu.roll(x, LANES - 32, 1), pltpu.roll(x, 32, 1))
    return x * cos + rot * sin


def _ret_kernel(lg_ref, q_ref, k_ref, v_ref, g_ref, gn_ref, *rest, T, tq, latent):
    if latent:
        cosq_ref, sinq_ref, cosk_ref, sink_ref, s0f_ref, s0b_ref, y_ref = rest
    else:
        y_ref, sf_ref, sb_ref = rest
    h = pl.program_id(1)
    qi = pl.program_id(2)
    lgf = lg_ref[0, h]
    lgb = lg_ref[1, h]
    q = q_ref[...]
    k = k_ref[...]
    v = v_ref[...]
    if latent:
        q = _rope(q, cosq_ref[...], sinq_ref[...])
        k = _rope(k, cosk_ref[...], sink_ref[...])
    k = k * (RET_D ** -0.5)
    t_idx = (lax.broadcasted_iota(jnp.int32, (tq, T), 0) + qi * tq).astype(F32)
    s_idx = lax.broadcasted_iota(jnp.int32, (tq, T), 1).astype(F32)
    diff = t_idx - s_idx
    dmask = (jnp.where(diff >= 0, jnp.exp(lgf * jnp.maximum(diff, 0.0)), 0.0)
             + jnp.where(diff <= 0, jnp.exp(lgb * jnp.maximum(-diff, 0.0)), 0.0))
    scores = _dot_nt(q, k) * dmask
    y = _dot(scores, v)
    if latent:
        tq_col = (lax.broadcasted_iota(jnp.int32, (tq, 1), 0) + qi * tq).astype(F32)
        y = y + _dot(q * jnp.exp(lgf * (tq_col + 1.0)), s0f_ref[...])
        y = y + _dot(q * jnp.exp(lgb * (T - tq_col)), s0b_ref[...])
    else:
        s_col = lax.broadcasted_iota(jnp.int32, (T, 1), 0).astype(F32)
        sf_ref[...] = _dot_tn(k * jnp.exp(lgf * (T - 1.0 - s_col)), v)
        sb_ref[...] = _dot_tn(k * jnp.exp(lgb * s_col), v)
    mu = jnp.mean(y, -1, keepdims=True)
    yc = y - mu
    var = jnp.mean(yc * yc, -1, keepdims=True)
    yn = yc * lax.rsqrt(var + 1e-5) * gn_ref[...]
    y_ref[...] = (yn * _silu(g_ref[...])).astype(y_ref.dtype)


def _ret_call(proj, lg, gn3, l, B, T, row0, y_out_rows, latent, extra):
    N = proj.shape[0]
    tq = 256
    nq = T // tq
    rb = row0 // T
    qmap = lambda b, h, qi: (rb * nq + b * nq + qi, h)
    kmap = lambda b, h, qi: (rb + b, RET_HEADS + h)
    vmap = lambda b, h, qi: (rb + b, 2 * RET_HEADS + h)
    gmap = lambda b, h, qi: (rb * nq + b * nq + qi, 3 * RET_HEADS + h)
    in_specs = [pl.BlockSpec(memory_space=pltpu.SMEM),
                pl.BlockSpec((tq, RET_D), qmap),
                pl.BlockSpec((T, RET_D), kmap),
                pl.BlockSpec((T, RET_D), vmap),
                pl.BlockSpec((tq, RET_D), gmap),
                pl.BlockSpec((None, 1, RET_D), lambda b, h, qi: (l, 0, h))]
    args = [lg, proj, proj, proj, proj, gn3]
    y_spec = pl.BlockSpec((tq, RET_D), lambda b, h, qi: (b * nq + qi, h))
    y_shape = jax.ShapeDtypeStruct((y_out_rows, RET_HEADS * RET_D), BF16)
    if latent:
        cos, sin, s0f, s0b = extra
        in_specs += [pl.BlockSpec((tq, RET_D), lambda b, h, qi: (qi, 0)),
                     pl.BlockSpec((tq, RET_D), lambda b, h, qi: (qi, 0)),
                     pl.BlockSpec((T, RET_D), lambda b, h, qi: (0, 0)),
                     pl.BlockSpec((T, RET_D), lambda b, h, qi: (0, 0)),
                     pl.BlockSpec((None, None, None, RET_D, RET_D), lambda b, h, qi: (b, l, h, 0, 0)),
                     pl.BlockSpec((None, None, None, RET_D, RET_D), lambda b, h, qi: (b, l, h, 0, 0))]
        args += [cos, sin, cos, sin, s0f, s0b]
        out_shape = y_shape
        out_specs = y_spec
    else:
        st = jax.ShapeDtypeStruct((B, RET_HEADS, RET_D, RET_D), F32)
        st_spec = pl.BlockSpec((None, None, RET_D, RET_D), lambda b, h, qi: (b, h, 0, 0))
        out_shape = (y_shape, st, st)
        out_specs = (y_spec, st_spec, st_spec)
    return pl.pallas_call(
        functools.partial(_ret_kernel, T=T, tq=tq, latent=latent),
        out_shape=out_shape,
        grid=(B, RET_HEADS, nq),
        in_specs=in_specs,
        out_specs=out_specs,
        compiler_params=_cparams(("arbitrary", "arbitrary", "arbitrary"), VMEM_LIMIT),
        name="retention_latent" if latent else "retention_ctx",
    )(*args)


def _rms(x, g):
    return x * lax.rsqrt(jnp.mean(x * x, -1, keepdims=True) + NORM_EPS) * g


def _na_ctx_kernel(q_ref, k_ref, v_ref, qn_ref, kn_ref, y_ref, ko_ref, vo_ref):
    q = _rms(q_ref[...], qn_ref[...]) * (NA_DH ** -0.5)
    k = _rms(k_ref[...], kn_ref[...])
    v = v_ref[...]
    s = _dot_nt(q, k)
    m = jnp.max(s, -1, keepdims=True)
    p = jnp.exp(s - m)
    den = jnp.sum(p, -1, keepdims=True)
    y_ref[...] = (_dot(p, v) / den).astype(y_ref.dtype)
    ko_ref[...] = k
    vo_ref[...] = v


def _na_ctx_call(proj, qn3, kn3, l, B, T):
    c0 = 4 * RET_HEADS
    kv = jax.ShapeDtypeStruct((B, NA_HEADS, T, NA_DH), F32)
    kv_spec = pl.BlockSpec((None, None, T, NA_DH), lambda b, h: (b, h, 0, 0))
    return pl.pallas_call(
        _na_ctx_kernel,
        out_shape=(jax.ShapeDtypeStruct((B * T, NA_HEADS * NA_DH), BF16), kv, kv),
        grid=(B, NA_HEADS),
        in_specs=[pl.BlockSpec((T, NA_DH), lambda b, h: (b, c0 + h)),
                  pl.BlockSpec((T, NA_DH), lambda b, h: (b, c0 + NA_HEADS + h)),
                  pl.BlockSpec((T, NA_DH), lambda b, h: (b, c0 + 2 * NA_HEADS + h)),
                  pl.BlockSpec((None, 1, NA_DH), lambda b, h: (l, 0, 0)),
                  pl.BlockSpec((None, 1, NA_DH), lambda b, h: (l, 0, 0))],
        out_specs=(pl.BlockSpec((T, NA_DH), lambda b, h: (b, h)), kv_spec, kv_spec),
        compiler_params=_cparams(("arbitrary", "arbitrary"), VMEM_LIMIT),
        name="na_ctx",
    )(proj, proj, proj, qn3, kn3)


def _na_lat_kernel(q_ref, k_ref, v_ref, ck_ref, cv_ref, bias_ref, qn_ref, kn_ref, y_ref, *, rows, kr):
    q = _rms(q_ref[...], qn_ref[...]) * (NA_DH ** -0.5)
    k = _rms(k_ref[...], kn_ref[...]).astype(BF16)
    v = v_ref[...].astype(BF16)
    ck = ck_ref[...].astype(BF16)
    cv = cv_ref[...].astype(BF16)
    nl = kr * GRID_W
    for r in range(rows):
        rs = min(max(r - kr // 2, 0), rows - kr)
        qr = q[r * GRID_W:(r + 1) * GRID_W, :].astype(BF16)
        kl = k[rs * GRID_W:rs * GRID_W + nl, :]
        vl = v[rs * GRID_W:rs * GRID_W + nl, :]
        s_loc = _dot_nt(qr, kl) + bias_ref[r]
        s_ctx = _dot_nt(qr, ck)
        m = jnp.maximum(jnp.max(s_loc, -1, keepdims=True), jnp.max(s_ctx, -1, keepdims=True))
        p_loc = jnp.exp(s_loc - m)
        p_ctx = jnp.exp(s_ctx - m)
        den = jnp.sum(p_loc, -1, keepdims=True) + jnp.sum(p_ctx, -1, keepdims=True)
        out = (_dot(p_loc, vl) + _dot(p_ctx, cv)) / den
        y_ref[r * GRID_W:(r + 1) * GRID_W, :] = out.astype(y_ref.dtype)


def _na_bias(rpb, rows):
    kr = min(NA_ROWS, rows)
    r_ids = jnp.arange(rows)
    row_start = jnp.clip(r_ids - kr // 2, 0, rows - kr)
    row_idx = row_start[:, None] + jnp.arange(kr)[None, :]
    cols = jnp.arange(GRID_W)
    col_start = jnp.clip(cols - NA_COLS // 2, 0, GRID_W - NA_COLS)
    col_ok = (cols[None, :] >= col_start[:, None]) & (cols[None, :] < col_start[:, None] + NA_COLS)
    dr_idx = row_idx - r_ids[:, None] + (NA_ROWS - 1)
    dc_idx = jnp.clip(cols[None, :] - cols[:, None] + (NA_COLS - 1), 0, 2 * NA_COLS - 2)
    bias = rpb[:, dr_idx[:, None, :, None], dc_idx[None, :, None, :]]
    bias = jnp.where(col_ok[None, None, :, None, :], bias.astype(F32), NEG_INF)
    return bias.reshape(rpb.shape[0], rows, GRID_W, kr * GRID_W)


def _na_lat_call(proj, cache_k, cache_v, bias, qn3, kn3, l, B, T, row0):
    c0 = 4 * RET_HEADS
    rb = row0 // T
    rows = T // GRID_W
    kr = min(NA_ROWS, rows)
    L = cache_k.shape[3]
    return pl.pallas_call(
        functools.partial(_na_lat_kernel, rows=rows, kr=kr),
        out_shape=jax.ShapeDtypeStruct((B * T, NA_HEADS * NA_DH), BF16),
        grid=(B, NA_HEADS),
        in_specs=[pl.BlockSpec((T, NA_DH), lambda b, h: (rb + b, c0 + h)),
                  pl.BlockSpec((T, NA_DH), lambda b, h: (rb + b, c0 + NA_HEADS + h)),
                  pl.BlockSpec((T, NA_DH), lambda b, h: (rb + b, c0 + 2 * NA_HEADS + h)),
                  pl.BlockSpec((None, None, None, L, NA_DH), lambda b, h: (b, l, h, 0, 0)),
                  pl.BlockSpec((None, None, None, L, NA_DH), lambda b, h: (b, l, h, 0, 0)),
                  pl.BlockSpec((None, rows, GRID_W, kr * GRID_W), lambda b, h: (h, 0, 0, 0)),
                  pl.BlockSpec((None, 1, NA_DH), lambda b, h: (l, 0, 0)),
                  pl.BlockSpec((None, 1, NA_DH), lambda b, h: (l, 0, 0))],
        out_specs=pl.BlockSpec((T, NA_DH), lambda b, h: (b, h)),
        compiler_params=_cparams(("arbitrary", "arbitrary"), VMEM_LIMIT),
        name="na_latent",
    )(proj, proj, proj, cache_k, cache_v, bias, qn3, kn3)


def _rw_prep_kernel(xr_ref, xk_ref, xv_ref, shr_ref, shk_ref, shv_ref, hl_ref, w2_ref, a2_ref, g2_ref,
                    w0_ref, a0_ref, kk_ref, ka_ref, rk_ref,
                    r_o, v_o, al_o, g_o, bon_o, lw0_o, be0_o, kd0_o, lw1_o, be1_o, kd1_o, *, T):
    row = lax.broadcasted_iota(jnp.int32, (T, 1), 0)

    def shift(x_ref, sh_ref):
        x = x_ref[...]
        prev = jnp.where(row == 0, 0.0, pltpu.roll(x, 1, 0))
        nxt = jnp.where(row == T - 1, 0.0, pltpu.roll(x, T - 1, 0))
        return sh_ref[0:1, :] * prev + sh_ref[1:2, :] * x + sh_ref[2:3, :] * nxt

    r = shift(xr_ref, shr_ref)
    k = shift(xk_ref, shk_ref)
    v = shift(xv_ref, shv_ref)
    p = _head_sum_matrix()
    hl = hl_ref[...]
    g = _dot(_sigmoid(hl[:, 4 * LANES:5 * LANES]), g2_ref[...])
    kk = k * kk_ref[...]
    kk = kk / jnp.maximum(jnp.sqrt(_seg_sum(kk * kk, p)), 1e-6)
    bonus = _seg_sum(r * k * rk_ref[...], p) * v
    r_o[...] = r
    v_o[...] = v
    al_o[...] = -kk
    g_o[...] = g
    bon_o[...] = bonus
    outs = ((lw0_o, be0_o, kd0_o), (lw1_o, be1_o, kd1_o))
    for d in range(2):
        z = w0_ref[d:d + 1, :] + _dot(jnp.tanh(hl[:, d * LANES:(d + 1) * LANES]), w2_ref[d])
        u = -z
        softplus = jnp.maximum(u, 0.0) + jnp.log(1.0 + jnp.exp(-jnp.abs(u)))
        wlog = -softplus - 0.5
        a = _sigmoid(a0_ref[d:d + 1, :] + _dot(hl[:, (2 + d) * LANES:(3 + d) * LANES], a2_ref[d]))
        lw_o, be_o, kd_o = outs[d]
        lw_o[...] = -jnp.exp(wlog)
        be_o[...] = kk * a
        kd_o[...] = k * (1.0 + (a - 1.0) * ka_ref[...])


def _rw_prep_call(proj, hl, shift3, w2p, a2p, g2, w0, a0, kk3, ka3, rk3, l, B, T, row0):
    rb = row0 // T
    nc = RW_W // RW_CB
    c0 = 7 * (RW_W // RW_CB)
    n_rows = B * T
    big = lambda off: pl.BlockSpec((T, RW_CB), lambda b, c: (rb + b, c0 + off * nc + c))
    sh = lambda off: pl.BlockSpec((None, 3, RW_CB), lambda b, c: (l, 0, off * nc + c))
    vec = pl.BlockSpec((None, 1, RW_CB), lambda b, c: (l, 0, c))
    out_spec = pl.BlockSpec((T, RW_CB), lambda b, c: (b, c))
    out = jax.ShapeDtypeStruct((n_rows, RW_W), F32)
    return pl.pallas_call(
        functools.partial(_rw_prep_kernel, T=T),
        out_shape=(out,) * 11,
        grid=(B, nc),
        in_specs=[big(0), big(1), big(2), sh(0), sh(1), sh(2),
                  pl.BlockSpec((T, LORA_W), lambda b, c: (rb + b, 0)),
                  pl.BlockSpec((None, 2, LANES, RW_CB), lambda b, c: (l, 0, 0, c)),
                  pl.BlockSpec((None, 2, LANES, RW_CB), lambda b, c: (l, 0, 0, c)),
                  pl.BlockSpec((None, LANES, RW_CB), lambda b, c: (l, 0, c)),
                  pl.BlockSpec((None, 2, RW_CB), lambda b, c: (l, 0, c)),
                  pl.BlockSpec((None, 2, RW_CB), lambda b, c: (l, 0, c)),
                  vec, vec, vec],
        out_specs=(out_spec,) * 11,
        compiler_params=_cparams(("arbitrary", "arbitrary"), VMEM_LIMIT),
        name="rwkv_prep",
    )(proj, proj, proj, shift3, shift3, shift3, hl, w2p, a2p, g2, w0, a0, kk3, ka3, rk3)


def _rw_scan_kernel(r_ref, v_ref, al_ref, lw_ref, be_ref, kd_ref, s0_ref, y_ref, sf_ref, st_ref,
                    *, reverse, nchunks):
    C = RW_CHUNK
    c = pl.program_id(1)

    @pl.when(c == 0)
    def _():
        st_ref[...] = s0_ref[...]

    ti = lax.broadcasted_iota(jnp.int32, (C, C), 0)
    si = lax.broadcasted_iota(jnp.int32, (C, C), 1)
    incl = (ti <= si) if reverse else (ti >= si)
    lw = lw_ref[...]
    cum = _dot_exact_lhs(incl.astype(F32), lw)
    tot = jnp.sum(lw, axis=0, keepdims=True)
    e_incl = jnp.exp(cum)
    e_excl = jnp.exp(cum - lw)
    e_neg = jnp.exp(-cum)
    g_tot = jnp.exp(tot)
    a_all = al_ref[...] * e_excl
    r_all = r_ref[...] * e_incl
    b_all = be_ref[...] * e_neg
    k_all = kd_ref[...] * e_neg
    v_all = v_ref[...]

    t2 = lax.broadcasted_iota(jnp.int32, (2 * C, 2 * C), 0)
    s2 = lax.broadcasted_iota(jnp.int32, (2 * C, 2 * C), 1)
    same = (t2 // C) == (s2 // C)
    if reverse:
        strict2 = same & (t2 < s2)
        incl2 = same & (t2 <= s2)
    else:
        strict2 = same & (t2 > s2)
        incl2 = same & (t2 >= s2)
    lane = lax.broadcasted_iota(jnp.int32, (C, LANES), 1)
    m0 = (lane < RW_N).astype(F32)
    m1 = 1.0 - m0

    def stack(x):
        return jnp.concatenate([x * m0, x * m1], axis=0).astype(BF16)

    n_steps = int(np.log2(C))
    for pr in range(RW_HEADS // 2):
        sl = slice(pr * LANES, (pr + 1) * LANES)
        a_s, r_s, b_s, k_s, v_s = (stack(x[:, sl]) for x in (a_all, r_all, b_all, k_all, v_all))
        sd = st_ref[pr]
        sd_b = sd.astype(BF16)
        m_ab = jnp.where(strict2, _dot_nt(a_s, b_s), 0.0)
        m_ak = jnp.where(strict2, _dot_nt(a_s, k_s), 0.0)
        m_rb = jnp.where(incl2, _dot_nt(r_s, b_s), 0.0)
        m_rk = jnp.where(incl2, _dot_nt(r_s, k_s), 0.0)
        x = _dot_nt(a_s, sd_b) + _dot(m_ak, v_s)
        pw = m_ab
        for step in range(n_steps):
            x = x + _dot(pw, x)
            if step + 1 < n_steps:
                pw = _dot(pw, pw)
        u_s = x.astype(BF16)
        y_s = _dot_nt(r_s, sd_b) + _dot(m_rb, u_s) + _dot(m_rk, v_s)
        y_ref[:, sl] = y_s[:C] + y_s[C:]
        st_ref[pr] = (sd + _dot_tn(u_s, b_s) + _dot_tn(v_s, k_s)) * g_tot[:, sl]

    @pl.when(c == nchunks - 1)
    def _():
        sf_ref[...] = st_ref[...]


def _rw_scan_call(r, v, al, lw, be, kd, s0, B, T, reverse):
    C = RW_CHUNK
    nch = T // C
    if reverse:
        cmap = lambda b, c: (b * nch + (nch - 1 - c), 0)
    else:
        cmap = lambda b, c: (b * nch + c, 0)
    blk = pl.BlockSpec((C, RW_W), cmap)
    st_spec = pl.BlockSpec((None, RW_HEADS // 2, LANES, LANES), lambda b, c: (b, 0, 0, 0))
    return pl.pallas_call(
        functools.partial(_rw_scan_kernel, reverse=reverse, nchunks=nch),
        out_shape=(jax.ShapeDtypeStruct((B * T, RW_W), F32),
                   jax.ShapeDtypeStruct((B, RW_HEADS // 2, LANES, LANES), F32)),
        grid=(B, nch),
        in_specs=[blk] * 6 + [st_spec],
        out_specs=(blk, st_spec),
        scratch_shapes=[pltpu.VMEM((RW_HEADS // 2, LANES, LANES), F32)],
        compiler_params=_cparams(("arbitrary", "arbitrary"), VMEM_LIMIT),
        name="rwkv_scan_bwd" if reverse else "rwkv_scan_fwd",
    )(r, v, al, lw, be, kd, s0)


def _rw_post_kernel(yf_ref, yb_ref, bon_ref, g_ref, gn_ref, o_ref):
    p = _head_sum_matrix()
    y = yf_ref[...] + yb_ref[...]
    mu = _seg_sum(y, p) * (1.0 / RW_N)
    yc = y - mu
    var = _seg_sum(yc * yc, p) * (1.0 / RW_N)
    yn = yc * lax.rsqrt(var + 64e-5) * gn_ref[...]
    o_ref[...] = ((yn + bon_ref[...]) * g_ref[...]).astype(o_ref.dtype)


def _rw_post_call(yf, yb, bonus, g, gn3, l):
    n_rows = yf.shape[0]
    tm = 256
    blk = pl.BlockSpec((tm, RW_W), lambda i: (i, 0))
    return pl.pallas_call(
        _rw_post_kernel,
        out_shape=jax.ShapeDtypeStruct((n_rows, RW_W), BF16),
        grid=(n_rows // tm,),
        in_specs=[blk, blk, blk, blk, pl.BlockSpec((None, 1, RW_W), lambda i: (l, 0, 0))],
        out_specs=blk,
        compiler_params=_cparams(("arbitrary",), VMEM_LIMIT),
        name="rwkv_post",
    )(yf, yb, bonus, g, gn3)


def _blockdiag_states(s):
    B = s.shape[0]
    s = s.reshape(B, RW_HEADS // 2, 2, RW_N, RW_N)
    z = jnp.zeros_like(s[:, :, 0])
    top = jnp.concatenate([s[:, :, 0], z], axis=-1)
    bot = jnp.concatenate([z, s[:, :, 1]], axis=-1)
    return jnp.concatenate([top, bot], axis=-2)


def _unblock_states(sd):
    B = sd.shape[0]
    h0 = sd[:, :, :RW_N, :RW_N]
    h1 = sd[:, :, RW_N:, RW_N:]
    return jnp.stack([h0, h1], axis=2).reshape(B, RW_HEADS, RW_N, RW_N)


def _merge_kernel(h_ref, ya_ref, yb_ref, yc_ref, wga_ref, wgb_ref, wgc_ref, bga_ref, bgb_ref, bgc_ref,
                  wa_ref, wb_ref, wc_ref, o_ref, wg_s, wbr_s):
    @pl.when(pl.program_id(1) == 0)
    def _():
        for n, (wg, wb) in enumerate(((wga_ref, wa_ref), (wgb_ref, wb_ref), (wgc_ref, wc_ref))):
            wg_s[n] = wg[...].astype(BF16)
            wbr_s[n] = wb[...].astype(BF16)

    h = h_ref[...]
    acc = None
    for n, (y_ref, bg_ref) in enumerate(((ya_ref, bga_ref), (yb_ref, bgb_ref), (yc_ref, bgc_ref))):
        gate = _sigmoid(jnp.dot(h, wg_s[n], preferred_element_type=F32) + bg_ref[...])
        term = gate * jnp.dot(y_ref[...], wbr_s[n], preferred_element_type=F32)
        acc = term if acc is None else acc + term
    o_ref[...] = acc.astype(o_ref.dtype)


def _merge_call(h, y_ret, y_na, y_rw, w_gate, b_gate3, w_ret, w_na, w_rw, l):
    N, D = h.shape
    KB = y_ret.shape[1]
    tm, tn = 256, 256
    nj = D // tn
    a_spec = pl.BlockSpec((tm, D), lambda j, i: (i, 0))
    y_spec = pl.BlockSpec((tm, KB), lambda j, i: (i, 0))
    wg = lambda n: pl.BlockSpec((None, D, tn), lambda j, i: (l, 0, n * nj + j))
    bg = lambda n: pl.BlockSpec((None, 1, tn), lambda j, i: (l, 0, n * nj + j))
    wb = pl.BlockSpec((None, KB, tn), lambda j, i: (l, 0, j))
    return pl.pallas_call(
        _merge_kernel,
        out_shape=jax.ShapeDtypeStruct((N, D), BF16),
        grid=(nj, N // tm),
        in_specs=[a_spec, y_spec, y_spec, y_spec, wg(0), wg(1), wg(2), bg(0), bg(1), bg(2), wb, wb, wb],
        out_specs=pl.BlockSpec((tm, tn), lambda j, i: (i, j)),
        scratch_shapes=[pltpu.VMEM((3, D, tn), BF16), pltpu.VMEM((3, KB, tn), BF16)],
        compiler_params=_cparams(("arbitrary", "arbitrary"), VMEM_LIMIT),
        name="gated_merge",
    )(h, y_ret, y_na, y_rw, w_gate, w_gate, w_gate, b_gate3, b_gate3, b_gate3, w_ret, w_na, w_rw)


def _out_kernel(a_ref, w_ref, x_ref, mod_ref, o_ref, wbf_ref, *, g_idx):
    @pl.when(pl.program_id(1) == 0)
    def _():
        wbf_ref[...] = w_ref[...].astype(BF16)

    m = jnp.dot(a_ref[...], wbf_ref[...], preferred_element_type=F32)
    o_ref[...] = x_ref[...] + mod_ref[g_idx:g_idx + 1, :] * m


def _out_call(a, w_out, x, mod, l, g_idx, n_prompt, t_s):
    N, D = x.shape
    tm, tn = ROW_TILE, 512
    row = _mod_row_map(tm, n_prompt, t_s)
    return pl.pallas_call(
        functools.partial(_out_kernel, g_idx=g_idx),
        out_shape=jax.ShapeDtypeStruct((N, D), F32),
        grid=(D // tn, N // tm),
        in_specs=[pl.BlockSpec((tm, D), lambda j, i: (i, 0)),
                  pl.BlockSpec((None, D, tn), lambda j, i: (l, 0, j)),
                  pl.BlockSpec((tm, tn), lambda j, i: (i, j)),
                  pl.BlockSpec((None, 6, tn), lambda j, i: (row(i), 0, j))],
        out_specs=pl.BlockSpec((tm, tn), lambda j, i: (i, j)),
        scratch_shapes=[pltpu.VMEM((D, tn), BF16)],
        compiler_params=_cparams(("arbitrary", "arbitrary"), VMEM_LIMIT),
        name="out_proj",
    )(a, w_out, x, mod)


def _router_kernel(h_ref, wt_ref, bias_ref, idx_ref, wgt_ref):
    h1, h2, h3 = _split3(h_ref[...])
    w1, w2, w3 = _split3(wt_ref[...])
    nt = lambda a, b: lax.dot_general(a, b, (((1,), (1,)), ((), ())), preferred_element_type=F32)
    logits = (nt(w1, h1) + nt(w1, h2) + nt(w2, h1) + nt(w1, h3) + nt(w2, h2) + nt(w3, h1))
    scores = _sigmoid(logits)
    sel = scores + bias_ref[...]
    row = lambda a, e: a[e:e + 1, :]
    grp = []
    for g in range(N_GROUPS):
        v = [row(sel, g * EXPERTS_PER_GROUP + j) for j in range(EXPERTS_PER_GROUP)]
        best = None
        for i in range(EXPERTS_PER_GROUP):
            for j in range(i + 1, EXPERTS_PER_GROUP):
                s = v[i] + v[j]
                best = s if best is None else jnp.maximum(best, s)
        grp.append(best)
    gbest = jnp.zeros_like(grp[0], dtype=jnp.int32)
    gval = grp[0]
    for g in range(1, N_GROUPS):
        better = grp[g] > gval
        gbest = jnp.where(better, g, gbest)
        gval = jnp.where(better, grp[g], gval)
    neg = jnp.full_like(gval, -jnp.inf)
    masked = [jnp.where(gbest == (e // EXPERTS_PER_GROUP), row(sel, e), neg) for e in range(N_EXPERTS)]

    def top(excl):
        bi = jnp.full_like(gbest, -1)
        bv = neg
        bs = jnp.zeros_like(gval)
        for e in range(N_EXPERTS):
            cand = masked[e] if excl is None else jnp.where(excl == e, neg, masked[e])
            better = cand > bv
            bi = jnp.where(better, e, bi)
            bv = jnp.where(better, cand, bv)
            bs = jnp.where(better, row(scores, e), bs)
        return bi, bs

    i1, s1 = top(None)
    i2, s2 = top(i1)
    den = s1 + s2
    idx_ref[...] = jnp.concatenate([i1, i2] + [jnp.zeros_like(i1)] * 6, axis=0)
    wgt_ref[...] = jnp.concatenate([s1 / den, s2 / den] + [jnp.zeros_like(s1)] * 6, axis=0)


def _router_call(h2, router_wt, router_bias2):
    N, D = h2.shape
    tm = 512
    return pl.pallas_call(
        _router_kernel,
        out_shape=(jax.ShapeDtypeStruct((8, N), jnp.int32), jax.ShapeDtypeStruct((8, N), F32)),
        grid=(N // tm,),
        in_specs=[pl.BlockSpec((tm, D), lambda i: (i, 0)),
                  pl.BlockSpec((N_EXPERTS, D), lambda i: (0, 0)),
                  pl.BlockSpec((N_EXPERTS, 1), lambda i: (0, 0))],
        out_specs=(pl.BlockSpec((8, tm), lambda i: (0, i)), pl.BlockSpec((8, tm), lambda i: (0, i))),
        compiler_params=_cparams(("arbitrary",), VMEM_LIMIT),
        name="moe_router",
    )(h2, router_wt, router_bias2)


def _row_copy(src_hbm, dst, sem, src_row, dst_row):
    return pltpu.make_async_copy(src_hbm.at[pl.ds(src_row, 1)], dst.at[pl.ds(dst_row, 1)], sem)


def _gather_kernel(tok_ref, x_hbm, o_ref, buf, sem, *, tm):
    base = pl.program_id(0) * tm

    def start(r, carry):
        _row_copy(x_hbm, buf, sem, tok_ref[base + r], r).start()
        return carry

    def wait(r, carry):
        _row_copy(x_hbm, buf, sem, 0, r).wait()
        return carry

    lax.fori_loop(0, tm, start, 0)
    lax.fori_loop(0, tm, wait, 0)
    o_ref[...] = buf[...].astype(o_ref.dtype)


def _gather_call(row_tok, x):
    P = row_tok.shape[0]
    D = x.shape[1]
    tm = MOE_TILE
    return pl.pallas_call(
        functools.partial(_gather_kernel, tm=tm),
        out_shape=jax.ShapeDtypeStruct((P, D), BF16),
        grid_spec=pltpu.PrefetchScalarGridSpec(
            num_scalar_prefetch=1,
            grid=(P // tm,),
            in_specs=[pl.BlockSpec(memory_space=pl.ANY)],
            out_specs=pl.BlockSpec((tm, D), lambda i, tok: (i, 0)),
            scratch_shapes=[pltpu.VMEM((tm, D), F32), pltpu.SemaphoreType.DMA(())]),
        compiler_params=_cparams(("arbitrary",), VMEM_LIMIT),
        name="moe_gather",
    )(row_tok, x)


def _expert_up_kernel(te_ref, x_ref, wg_ref, wu_ref, o_ref, wg_s, wu_s):
    i = pl.program_id(1)
    changed = jnp.logical_or(i == 0, te_ref[i] != te_ref[jnp.maximum(i - 1, 0)])

    @pl.when(changed)
    def _():
        wg_s[...] = wg_ref[...].astype(BF16)
        wu_s[...] = wu_ref[...].astype(BF16)

    x = x_ref[...]
    a = jnp.dot(x, wg_s[...], preferred_element_type=F32)
    b = jnp.dot(x, wu_s[...], preferred_element_type=F32)
    o_ref[...] = (_silu(a) * b).astype(o_ref.dtype)


def _expert_up_call(tile_exp, xs, w_gate, w_up, l):
    P, D = xs.shape
    FF = w_gate.shape[-1]
    tm, tf = MOE_TILE, 256
    w_spec = pl.BlockSpec((None, None, D, tf), lambda f, i, te: (l, te[i], 0, f))
    return pl.pallas_call(
        _expert_up_kernel,
        out_shape=jax.ShapeDtypeStruct((P, FF), BF16),
        grid_spec=pltpu.PrefetchScalarGridSpec(
            num_scalar_prefetch=1,
            grid=(FF // tf, P // tm),
            in_specs=[pl.BlockSpec((tm, D), lambda f, i, te: (i, 0)), w_spec, w_spec],
            out_specs=pl.BlockSpec((tm, tf), lambda f, i, te: (i, f)),
            scratch_shapes=[pltpu.VMEM((D, tf), BF16), pltpu.VMEM((D, tf), BF16)]),
        compiler_params=_cparams(("arbitrary", "arbitrary"), VMEM_LIMIT),
        name="moe_expert_up",
    )(tile_exp, xs, w_gate, w_up)


def _expert_down_kernel(te_ref, h_ref, wd_ref, o_ref, wd_s):
    i = pl.program_id(1)
    changed = jnp.logical_or(i == 0, te_ref[i] != te_ref[jnp.maximum(i - 1, 0)])

    @pl.when(changed)
    def _():
        wd_s[...] = wd_ref[...].astype(BF16)

    o_ref[...] = jnp.dot(h_ref[...], wd_s[...], preferred_element_type=F32)


def _expert_down_call(tile_exp, hmid, w_down, l):
    P, FF = hmid.shape
    D = w_down.shape[-1]
    tm, tn = MOE_TILE, 1024
    return pl.pallas_call(
        _expert_down_kernel,
        out_shape=jax.ShapeDtypeStruct((P, D), F32),
        grid_spec=pltpu.PrefetchScalarGridSpec(
            num_scalar_prefetch=1,
            grid=(D // tn, P // tm),
            in_specs=[pl.BlockSpec((tm, FF), lambda n, i, te: (i, 0)),
                      pl.BlockSpec((None, None, FF, tn), lambda n, i, te: (l, te[i], 0, n))],
            out_specs=pl.BlockSpec((tm, tn), lambda n, i, te: (i, n)),
            scratch_shapes=[pltpu.VMEM((FF, tn), BF16)]),
        compiler_params=_cparams(("arbitrary", "arbitrary"), VMEM_LIMIT),
        name="moe_expert_down",
    )(tile_exp, hmid, w_down)


def _combine_kernel(pos_ref, ys_hbm, x_ref, w_ref, mod_ref, o_ref, buf_a, buf_b, sem, *, tm, g_idx, n_tok):
    base = pl.program_id(0) * tm

    def start(r, carry):
        _row_copy(ys_hbm, buf_a, sem.at[0], pos_ref[base + r], r).start()
        _row_copy(ys_hbm, buf_b, sem.at[1], pos_ref[n_tok + base + r], r).start()
        return carry

    def wait(r, carry):
        _row_copy(ys_hbm, buf_a, sem.at[0], 0, r).wait()
        _row_copy(ys_hbm, buf_b, sem.at[1], 0, r).wait()
        return carry

    lax.fori_loop(0, tm, start, 0)
    lax.fori_loop(0, tm, wait, 0)
    w = w_ref[...]
    y = w[:, 0:1] * buf_a[...] + w[:, 1:2] * buf_b[...]
    o_ref[...] = x_ref[...] + mod_ref[g_idx:g_idx + 1, :] * y


def _combine_call(pos2, ys, x, w_cols, mod, g_idx, n_prompt, t_s):
    N, D = x.shape
    tm = MOE_TILE
    row = _mod_row_map(tm, n_prompt, t_s)
    return pl.pallas_call(
        functools.partial(_combine_kernel, tm=tm, g_idx=g_idx, n_tok=N),
        out_shape=jax.ShapeDtypeStruct((N, D), F32),
        grid_spec=pltpu.PrefetchScalarGridSpec(
            num_scalar_prefetch=1,
            grid=(N // tm,),
            in_specs=[pl.BlockSpec(memory_space=pl.ANY),
                      pl.BlockSpec((tm, D), lambda i, pos: (i, 0)),
                      pl.BlockSpec((tm, LANES), lambda i, pos: (i, 0)),
                      pl.BlockSpec((None, 6, D), lambda i, pos: (row(i), 0, 0))],
            out_specs=pl.BlockSpec((tm, D), lambda i, pos: (i, 0)),
            scratch_shapes=[pltpu.VMEM((tm, D), F32), pltpu.VMEM((tm, D), F32),
                            pltpu.SemaphoreType.DMA((2,))]),
        compiler_params=_cparams(("arbitrary",), VMEM_LIMIT),
        name="moe_combine",
    )(pos2, ys, x, w_cols, mod)


def _moe_plan(idx, n_tiles):
    N = idx.shape[1]
    tm = MOE_TILE
    e_flat = idx.reshape(-1)
    onehot = (e_flat[:, None] == jnp.arange(N_EXPERTS)[None, :]).astype(jnp.int32)
    rank = jnp.take_along_axis(jnp.cumsum(onehot, axis=0), e_flat[:, None], axis=1)[:, 0] - 1
    counts = jnp.sum(onehot, axis=0)
    tiles = (counts + tm - 1) // tm
    tile_end = jnp.cumsum(tiles)
    tile_start = tile_end - tiles
    pos = tile_start[e_flat] * tm + rank
    tok = jnp.tile(jnp.arange(N, dtype=jnp.int32), 2)
    row_tok = jnp.zeros((n_tiles * tm,), jnp.int32).at[pos].set(tok)
    tile_exp = jnp.minimum(jnp.searchsorted(tile_end, jnp.arange(n_tiles), side="right"),
                           N_EXPERTS - 1).astype(jnp.int32)
    return row_tok, tile_exp, pos.astype(jnp.int32)


def _moe(h2f, x, mod, router_wt, router_bias2, w_gate, w_up, w_down, l, n_prompt, t_s):
    N = x.shape[0]
    idx8, wgt8 = _router_call(h2f, router_wt, router_bias2)
    n_tiles = (2 * N) // MOE_TILE + N_EXPERTS
    row_tok, tile_exp, pos2 = _moe_plan(idx8[:2], n_tiles)
    xs = _gather_call(row_tok, h2f)
    hmid = _expert_up_call(tile_exp, xs, w_gate, w_up, l)
    ys = _expert_down_call(tile_exp, hmid, w_down, l)
    w_cols = jnp.zeros((N, LANES), F32).at[:, 0].set(wgt8[0]).at[:, 1].set(wgt8[1])
    return _combine_call(pos2, ys, x, w_cols, mod, 5, n_prompt, t_s)


def _rope_tables(T):
    nf = RET_D // 4
    t = jnp.arange(T)
    pos = jnp.stack([t // GRID_W, t % GRID_W], -1).astype(F32)
    inv = ROPE_BASE ** (-jnp.arange(nf, dtype=F32) / nf)
    ang = pos[:, :, None] * inv
    cos, sin = jnp.cos(ang), jnp.sin(ang)
    cos_t = jnp.concatenate([cos, cos], axis=-1).reshape(T, RET_D)
    sin_t = jnp.concatenate([-sin, sin], axis=-1).reshape(T, RET_D)
    return cos_t, sin_t


def kernel(x_prompt, x_sample, state_ret_fwd, state_ret_bwd, cache_na_k, cache_na_v, state_rwkv_fwd, state_rwkv_bwd, c, c_ctx, w_mod, b_mod, norm_mix, norm_ffn, w_in, w_gate, b_gate, w_br_ret, w_br_na, w_br_rw, w_out, ret_log_decay, ret_gn, na_qn, na_kn, na_rpb, rw_shift, rw_w0, rw_w1, rw_w2, rw_a0, rw_a1, rw_a2, rw_g1, rw_g2, rw_kk, rw_ka, rw_rk, rw_gn, router_w, router_bias, moe_w_gate, moe_w_up, moe_w_down):
    BP, TP, D = x_prompt.shape
    BS, TS, _ = x_sample.shape
    depth = w_in.shape[0]
    NP, NS = BP * TP, BS * TS
    N = NP + NS
    assert NP % ROW_TILE == 0 and TS % ROW_TILE == 0 and TP % RW_CHUNK == 0 and TS % RW_CHUNK == 0
    assert TS % TP == 0 and NP % TS == 0 and BS <= 7

    x = jnp.concatenate([x_prompt.reshape(NP, D), x_sample.reshape(NS, D)], axis=0)
    c8 = jnp.concatenate([c_ctx[None, :], c, jnp.zeros((7 - BS, D), F32)], axis=0)
    b_mod3 = b_mod.reshape(depth, 1, 6 * D)
    norm_mix3 = norm_mix.reshape(depth, 1, D)
    norm_ffn3 = norm_ffn.reshape(depth, 1, D)
    b_gate3 = b_gate.reshape(depth, 1, 3 * D)
    ret_gn3 = ret_gn.reshape(depth, 1, RET_HEADS * RET_D)
    na_qn3 = na_qn.reshape(depth, 1, NA_DH)
    na_kn3 = na_kn.reshape(depth, 1, NA_DH)
    rw_kk3 = rw_kk.reshape(depth, 1, RW_W)
    rw_ka3 = rw_ka.reshape(depth, 1, RW_W)
    rw_rk3 = rw_rk.reshape(depth, 1, RW_W)
    rw_gn3 = rw_gn.reshape(depth, 1, RW_W)
    zpad = jnp.zeros((depth, D, LANES - 64), F32)
    w_lora = jnp.concatenate([rw_w1[:, 0], zpad, rw_w1[:, 1], zpad, rw_a1[:, 0], zpad, rw_a1[:, 1], zpad,
                              rw_g1], axis=-1)
    kpad = jnp.zeros((depth, 2, LANES - 64, RW_W), F32)
    w2p = jnp.concatenate([rw_w2, kpad], axis=2)
    a2p = jnp.concatenate([rw_a2, kpad], axis=2)
    router_wt = router_w.T
    router_bias2 = router_bias.reshape(N_EXPERTS, 1)
    cos_t, sin_t = _rope_tables(TS)
    zero_st = jnp.zeros((BP, RW_HEADS // 2, LANES, LANES), F32)

    outs = {k: [] for k in ("ret_f", "ret_b", "na_k", "na_v", "rw_f", "rw_b")}
    for l in range(depth):
        mod = _mod_call(c8, w_mod, b_mod3, l)[:1 + BS].reshape(1 + BS, 6, D)
        (h,) = _norm_mod_call(x, norm_mix3, mod, l, 0, 1, NP, TS, (BF16,))
        proj = _mm_call(h, w_in, l, 512, "in_proj")
        hl = _mm_call(h, w_lora, l, LORA_W, "lora_proj")

        y_ret_p, sf, sb = _ret_call(proj, ret_log_decay[l], ret_gn3, l, BP, TP, 0, NP, False, None)
        y_ret_s = _ret_call(proj, ret_log_decay[l], ret_gn3, l, BS, TS, NP, NS, True,
                            (cos_t, sin_t, state_ret_fwd, state_ret_bwd))
        outs["ret_f"].append(sf)
        outs["ret_b"].append(sb)
        y_na_p, nk, nv = _na_ctx_call(proj, na_qn3, na_kn3, l, BP, TP)
        bias = _na_bias(na_rpb[l], TS // GRID_W)
        y_na_s = _na_lat_call(proj, cache_na_k, cache_na_v, bias, na_qn3, na_kn3, l, BS, TS, NP)
        outs["na_k"].append(nk)
        outs["na_v"].append(nv)
        y_rw = []
        for (B, T, row0, s0f, s0b) in ((BP, TP, 0, zero_st, zero_st),
                                       (BS, TS, NP, _blockdiag_states(state_rwkv_fwd[:, l]),
                                        _blockdiag_states(state_rwkv_bwd[:, l]))):
            (r, v, al, g, bonus, lw0, be0, kd0, lw1, be1, kd1) = _rw_prep_call(
                proj, hl, rw_shift, w2p, a2p, rw_g2, rw_w0, rw_a0, rw_kk3, rw_ka3, rw_rk3, l, B, T, row0)
            yf, stf = _rw_scan_call(r, v, al, lw0, be0, kd0, s0f, B, T, False)
            yb, stb = _rw_scan_call(r, v, al, lw1, be1, kd1, s0b, B, T, True)
            y_rw.append(_rw_post_call(yf, yb, bonus, g, rw_gn3, l))
            if row0 == 0:
                outs["rw_f"].append(_unblock_states(stf))
                outs["rw_b"].append(_unblock_states(stb))
        y_ret = jnp.concatenate([y_ret_p, y_ret_s], axis=0)
        y_na = jnp.concatenate([y_na_p, y_na_s], axis=0)
        y_rwc = jnp.concatenate(y_rw, axis=0)

        merged = _merge_call(h, y_ret, y_na, y_rwc, w_gate, b_gate3, w_br_ret, w_br_na, w_br_rw, l)
        x = _out_call(merged, w_out, x, mod, l, 2, NP, TS)
        (h2f,) = _norm_mod_call(x, norm_ffn3, mod, l, 3, 4, NP, TS, (F32,))
        x = _moe(h2f, x, mod, router_wt, router_bias2, moe_w_gate, moe_w_up, moe_w_down, l, NP, TS)

    y_p = x[:NP].reshape(BP, TP, D)
    y_s = x[NP:].reshape(BS, TS, D)
    st = lambda k: jnp.stack(outs[k], axis=1)
    return (y_p, y_s, st("ret_f"), st("ret_b"), st("na_k"), st("na_v"), st("rw_f"), st("rw_b"))
```

```python
import functools

import jax
import jax.numpy as jnp
import numpy as np
from jax import lax
from jax.experimental import pallas as pl
from jax.experimental.pallas import tpu as pltpu

F32 = jnp.float32
BF16 = jnp.bfloat16

GRID_W = 64
RET_HEADS = 8
RET_D = 128
NA_HEADS = 8
NA_DH = 128
NA_ROWS = 8
NA_COLS = 16
RW_HEADS = 16
RW_N = 64
RW_W = RW_HEADS * RW_N
N_EXPERTS = 16
N_GROUPS = 4
EXPERTS_PER_GROUP = 4
ROPE_BASE = 10000.0
NORM_EPS = 1e-6
NEG_INF = -1e30

LANES = 128
RW_CHUNK = 64
RW_CB = 256
LORA_W = 640
ROW_TILE = 512
MOE_TILE = 256
VMEM_LIMIT = 56 * 1024 * 1024


def _cparams(sem, vmem=None):
    return pltpu.CompilerParams(dimension_semantics=sem, vmem_limit_bytes=vmem)


def _dot(a, b):
    return jnp.dot(a.astype(BF16), b.astype(BF16), preferred_element_type=F32)


def _dot_nt(a, b):
    return lax.dot_general(a.astype(BF16), b.astype(BF16), (((1,), (1,)), ((), ())),
                           preferred_element_type=F32)


def _dot_tn(a, b):
    return lax.dot_general(a.astype(BF16), b.astype(BF16), (((0,), (0,)), ((), ())),
                           preferred_element_type=F32)


def _split3(x):
    h = x.astype(BF16)
    r1 = x - h.astype(F32)
    m = r1.astype(BF16)
    l = (r1 - m.astype(F32)).astype(BF16)
    return h, m, l


def _dot_exact_lhs(c, x):
    h, m, l = _split3(x)
    cb = c.astype(BF16)
    return (jnp.dot(cb, h, preferred_element_type=F32) + jnp.dot(cb, m, preferred_element_type=F32)
            + jnp.dot(cb, l, preferred_element_type=F32))


def _dot_exact_rhs(x, c):
    h, m, l = _split3(x)
    cb = c.astype(BF16)
    return (jnp.dot(h, cb, preferred_element_type=F32) + jnp.dot(m, cb, preferred_element_type=F32)
            + jnp.dot(l, cb, preferred_element_type=F32))


def _head_sum_matrix():
    r = lax.broadcasted_iota(jnp.int32, (LANES, LANES), 0) // RW_N
    c = lax.broadcasted_iota(jnp.int32, (LANES, LANES), 1) // RW_N
    return (r == c).astype(F32)


def _seg_sum(x, p):
    parts = [_dot_exact_rhs(x[:, s * LANES:(s + 1) * LANES], p) for s in range(x.shape[1] // LANES)]
    return parts[0] if len(parts) == 1 else jnp.concatenate(parts, axis=1)


def _sigmoid(x):
    return 1.0 / (1.0 + jnp.exp(-x))


def _silu(x):
    return x * _sigmoid(x)


def _mod_kernel(c_ref, w_ref, b_ref, o_ref):
    a = _silu(c_ref[...])
    o_ref[...] = _dot(a, w_ref[...]) + b_ref[...]


def _mod_call(c8, w_mod, b_mod3, l):
    D = c8.shape[1]
    n_out = w_mod.shape[2]
    tn = 512
    return pl.pallas_call(
        _mod_kernel,
        out_shape=jax.ShapeDtypeStruct((8, n_out), F32),
        grid=(n_out // tn,),
        in_specs=[pl.BlockSpec((8, D), lambda j: (0, 0)),
                  pl.BlockSpec((None, D, tn), lambda j: (l, 0, j)),
                  pl.BlockSpec((None, 1, tn), lambda j: (l, 0, j))],
        out_specs=pl.BlockSpec((8, tn), lambda j: (0, j)),
        compiler_params=_cparams(("arbitrary",), VMEM_LIMIT),
        name="mod_proj",
    )(c8, w_mod, b_mod3)


def _norm_mod_kernel(x_ref, g_ref, mod_ref, *out_refs, sh_idx, sc_idx):
    x = x_ref[...]
    y = x * lax.rsqrt(jnp.mean(x * x, -1, keepdims=True) + NORM_EPS) * g_ref[...]
    h = y * (1.0 + mod_ref[sc_idx:sc_idx + 1, :]) + mod_ref[sh_idx:sh_idx + 1, :]
    for o in out_refs:
        o[...] = h.astype(o.dtype)


def _mod_row_map(tm, n_prompt, t_s):
    def row(i):
        r = i * tm
        return jnp.where(r < n_prompt, 0, 1 + (r - n_prompt) // t_s)
    return row


def _norm_mod_call(x, g3, mod, l, sh_idx, sc_idx, n_prompt, t_s, out_dtypes):
    N, D = x.shape
    tm = 256
    row = _mod_row_map(tm, n_prompt, t_s)
    outs = tuple(jax.ShapeDtypeStruct((N, D), dt) for dt in out_dtypes)
    return pl.pallas_call(
        functools.partial(_norm_mod_kernel, sh_idx=sh_idx, sc_idx=sc_idx),
        out_shape=outs,
        grid=(N // tm,),
        in_specs=[pl.BlockSpec((tm, D), lambda i: (i, 0)),
                  pl.BlockSpec((None, 1, D), lambda i: (l, 0, 0)),
                  pl.BlockSpec((None, 6, D), lambda i: (row(i), 0, 0))],
        out_specs=tuple(pl.BlockSpec((tm, D), lambda i: (i, 0)) for _ in out_dtypes),
        compiler_params=_cparams(("arbitrary",), VMEM_LIMIT),
        name="norm_mod",
    )(x, g3, mod)


def _mm_kernel(a_ref, w_ref, o_ref, wbf_ref):
    @pl.when(pl.program_id(1) == 0)
    def _():
        wbf_ref[...] = w_ref[...].astype(BF16)

    o_ref[...] = jnp.dot(a_ref[...], wbf_ref[...], preferred_element_type=F32).astype(o_ref.dtype)


def _mm_call(a, w, l, tm, tn, name, out_dtype=F32):
    N, K = a.shape
    M = w.shape[-1]
    if w.ndim == 3:
        w_spec = pl.BlockSpec((None, K, tn), lambda j, i: (l, 0, j))
    else:
        w_spec = pl.BlockSpec((K, tn), lambda j, i: (0, j))
    return pl.pallas_call(
        _mm_kernel,
        out_shape=jax.ShapeDtypeStruct((N, M), out_dtype),
        grid=(M // tn, N // tm),
        in_specs=[pl.BlockSpec((tm, K), lambda j, i: (i, 0)), w_spec],
        out_specs=pl.BlockSpec((tm, tn), lambda j, i: (i, j)),
        scratch_shapes=[pltpu.VMEM((K, tn), BF16)],
        compiler_params=_cparams(("arbitrary", "arbitrary"), VMEM_LIMIT),
        name=name,
    )(a, w)


def _rope(x, cos, sin):
    lane = lax.broadcasted_iota(jnp.int32, x.shape, 1)
    first = (lane % 64) < 32
    rot = jnp.where(first, pltpu.roll(x, LANES - 32, 1), pltpu.roll(x, 32, 1))
    return x * cos + rot * sin


def _ret_kernel(lg_ref, q_ref, k_ref, v_ref, g_ref, gn_ref, *rest, T, tq, latent):
    if latent:
        cosq_ref, sinq_ref, cosk_ref, sink_ref, s0f_ref, s0b_ref, _, y_ref = rest
    else:
        y_ref, sf_ref, sb_ref = rest[-3:]
    h = pl.program_id(1)
    qi = pl.program_id(2)
    lgf = lg_ref[0, h]
    lgb = lg_ref[1, h]
    q = q_ref[...]
    k = k_ref[...]
    v = v_ref[...]
    if latent:
        q = _rope(q, cosq_ref[...], sinq_ref[...])
        k = _rope(k, cosk_ref[...], sink_ref[...])
    k = k * (RET_D ** -0.5)
    t_idx = (lax.broadcasted_iota(jnp.int32, (tq, T), 0) + qi * tq).astype(F32)
    s_idx = lax.broadcasted_iota(jnp.int32, (tq, T), 1).astype(F32)
    diff = t_idx - s_idx
    dmask = (jnp.where(diff >= 0, jnp.exp(lgf * jnp.maximum(diff, 0.0)), 0.0)
             + jnp.where(diff <= 0, jnp.exp(lgb * jnp.maximum(-diff, 0.0)), 0.0))
    scores = _dot_nt(q, k) * dmask
    y = _dot(scores, v)
    if latent:
        tq_col = (lax.broadcasted_iota(jnp.int32, (tq, 1), 0) + qi * tq).astype(F32)
        y = y + _dot(q * jnp.exp(lgf * (tq_col + 1.0)), s0f_ref[...])
        y = y + _dot(q * jnp.exp(lgb * (T - tq_col)), s0b_ref[...])
    else:
        s_col = lax.broadcasted_iota(jnp.int32, (T, 1), 0).astype(F32)
        sf_ref[...] = _dot_tn(k * jnp.exp(lgf * (T - 1.0 - s_col)), v)
        sb_ref[...] = _dot_tn(k * jnp.exp(lgb * s_col), v)
    mu = jnp.mean(y, -1, keepdims=True)
    yc = y - mu
    var = jnp.mean(yc * yc, -1, keepdims=True)
    yn = yc * lax.rsqrt(var + 1e-5) * gn_ref[...]
    y_ref[...] = (yn * _silu(g_ref[...])).astype(y_ref.dtype)


def _ret_call(proj, lg, gn3, l, B, T, row0, latent, extra):
    N = proj.shape[0]
    tq = 256
    nq = T // tq
    rb = row0 // T
    qmap = lambda b, h, qi: (rb * nq + b * nq + qi, h)
    kmap = lambda b, h, qi: (rb + b, RET_HEADS + h)
    vmap = lambda b, h, qi: (rb + b, 2 * RET_HEADS + h)
    gmap = lambda b, h, qi: (rb * nq + b * nq + qi, 3 * RET_HEADS + h)
    in_specs = [pl.BlockSpec(memory_space=pltpu.SMEM),
                pl.BlockSpec((tq, RET_D), qmap),
                pl.BlockSpec((T, RET_D), kmap),
                pl.BlockSpec((T, RET_D), vmap),
                pl.BlockSpec((tq, RET_D), gmap),
                pl.BlockSpec((None, 1, RET_D), lambda b, h, qi: (l, 0, h))]
    args = [lg, proj, proj, proj, proj, gn3]
    y_spec = pl.BlockSpec((tq, RET_D), qmap)
    y_shape = jax.ShapeDtypeStruct((N, RET_HEADS * RET_D), BF16)
    if latent:
        cos, sin, s0f, s0b, y_buf = extra
        in_specs += [pl.BlockSpec((tq, RET_D), lambda b, h, qi: (qi, 0)),
                     pl.BlockSpec((tq, RET_D), lambda b, h, qi: (qi, 0)),
                     pl.BlockSpec((T, RET_D), lambda b, h, qi: (0, 0)),
                     pl.BlockSpec((T, RET_D), lambda b, h, qi: (0, 0)),
                     pl.BlockSpec((None, None, None, RET_D, RET_D), lambda b, h, qi: (b, l, h, 0, 0)),
                     pl.BlockSpec((None, None, None, RET_D, RET_D), lambda b, h, qi: (b, l, h, 0, 0)),
                     pl.BlockSpec(memory_space=pl.ANY)]
        args += [cos, sin, cos, sin, s0f, s0b, y_buf]
        aliases = {len(args) - 1: 0}
        out_shape = y_shape
        out_specs = y_spec
    else:
        y_buf, sf_buf, sb_buf = extra
        st = jax.ShapeDtypeStruct(sf_buf.shape, F32)
        st_spec = pl.BlockSpec((None, None, None, RET_D, RET_D), lambda b, h, qi: (b, l, h, 0, 0))
        in_specs += [pl.BlockSpec(memory_space=pl.ANY)] * 3
        args += [y_buf, sf_buf, sb_buf]
        aliases = {len(args) - 3: 0, len(args) - 2: 1, len(args) - 1: 2}
        out_shape = (y_shape, st, st)
        out_specs = (y_spec, st_spec, st_spec)
    return pl.pallas_call(
        functools.partial(_ret_kernel, T=T, tq=tq, latent=latent),
        out_shape=out_shape,
        grid=(B, RET_HEADS, nq),
        in_specs=in_specs,
        out_specs=out_specs,
        input_output_aliases=aliases,
        compiler_params=_cparams(("arbitrary", "arbitrary", "arbitrary"), VMEM_LIMIT),
        name="retention_latent" if latent else "retention_ctx",
    )(*args)


def _rms(x, g):
    return x * lax.rsqrt(jnp.mean(x * x, -1, keepdims=True) + NORM_EPS) * g


def _na_ctx_kernel(q_ref, k_ref, v_ref, qn_ref, kn_ref, *rest):
    y_ref, ko_ref, vo_ref = rest[-3:]
    q = _rms(q_ref[...], qn_ref[...]) * (NA_DH ** -0.5)
    k = _rms(k_ref[...], kn_ref[...])
    v = v_ref[...]
    s = _dot_nt(q, k)
    m = jnp.max(s, -1, keepdims=True)
    p = jnp.exp(s - m)
    den = jnp.sum(p, -1, keepdims=True)
    y_ref[...] = (_dot(p, v) / den).astype(y_ref.dtype)
    ko_ref[...] = k
    vo_ref[...] = v


def _na_ctx_call(proj, qn3, kn3, y_buf, k_buf, v_buf, l, B, T):
    N = proj.shape[0]
    c0 = 4 * RET_HEADS
    kv = jax.ShapeDtypeStruct(k_buf.shape, F32)
    kv_spec = pl.BlockSpec((None, None, None, T, NA_DH), lambda b, h: (b, l, h, 0, 0))
    any_spec = pl.BlockSpec(memory_space=pl.ANY)
    return pl.pallas_call(
        _na_ctx_kernel,
        out_shape=(jax.ShapeDtypeStruct((N, NA_HEADS * NA_DH), BF16), kv, kv),
        grid=(B, NA_HEADS),
        in_specs=[pl.BlockSpec((T, NA_DH), lambda b, h: (b, c0 + h)),
                  pl.BlockSpec((T, NA_DH), lambda b, h: (b, c0 + NA_HEADS + h)),
                  pl.BlockSpec((T, NA_DH), lambda b, h: (b, c0 + 2 * NA_HEADS + h)),
                  pl.BlockSpec((None, 1, NA_DH), lambda b, h: (l, 0, 0)),
                  pl.BlockSpec((None, 1, NA_DH), lambda b, h: (l, 0, 0)),
                  any_spec, any_spec, any_spec],
        out_specs=(pl.BlockSpec((T, NA_DH), lambda b, h: (b, h)), kv_spec, kv_spec),
        input_output_aliases={5: 0, 6: 1, 7: 2},
        compiler_params=_cparams(("arbitrary", "arbitrary"), VMEM_LIMIT),
        name="na_ctx",
    )(proj, proj, proj, qn3, kn3, y_buf, k_buf, v_buf)


def _na_lat_kernel(q_ref, k_ref, v_ref, ck_ref, cv_ref, bias_ref, qn_ref, kn_ref, _, y_ref, *, rows, kr):
    q = _rms(q_ref[...], qn_ref[...]) * (NA_DH ** -0.5)
    k = _rms(k_ref[...], kn_ref[...]).astype(BF16)
    v = v_ref[...].astype(BF16)
    ck = ck_ref[...].astype(BF16)
    cv = cv_ref[...].astype(BF16)
    nl = kr * GRID_W
    qb = q.astype(BF16)
    starts = [min(max(r - kr // 2, 0), rows - kr) for r in range(rows)]
    groups = []
    for r, rs in enumerate(starts):
        if groups and groups[-1][0] == rs:
            groups[-1][2] = r + 1
        else:
            groups.append([rs, r, r + 1])
    W = GRID_W
    s_ctx = _dot_nt(qb, ck)
    s_loc = [_dot_nt(qb[r0 * W:r1 * W], k[rs * W:rs * W + nl])
             + bias_ref[r0:r1].reshape((r1 - r0) * W, nl) for rs, r0, r1 in groups]
    p_loc, p_ctx, den = [], [], []
    for (rs, r0, r1), sl in zip(groups, s_loc):
        sc = s_ctx[r0 * W:r1 * W]
        m = jnp.maximum(jnp.max(sl, -1, keepdims=True), jnp.max(sc, -1, keepdims=True))
        pl_, pc_ = jnp.exp(sl - m), jnp.exp(sc - m)
        p_loc.append(pl_.astype(BF16))
        p_ctx.append(pc_.astype(BF16))
        den.append(jnp.sum(pl_, -1, keepdims=True) + jnp.sum(pc_, -1, keepdims=True))
    o_ctx = _dot(jnp.concatenate(p_ctx, axis=0), cv)
    o_loc = [_dot(p, v[rs * W:rs * W + nl]) for (rs, r0, r1), p in zip(groups, p_loc)]
    for (rs, r0, r1), ol, d in zip(groups, o_loc, den):
        y_ref[r0 * W:r1 * W, :] = ((ol + o_ctx[r0 * W:r1 * W]) / d).astype(y_ref.dtype)


def _na_bias(rpb, rows):
    kr = min(NA_ROWS, rows)
    r_ids = jnp.arange(rows)
    row_start = jnp.clip(r_ids - kr // 2, 0, rows - kr)
    row_idx = row_start[:, None] + jnp.arange(kr)[None, :]
    cols = jnp.arange(GRID_W)
    col_start = jnp.clip(cols - NA_COLS // 2, 0, GRID_W - NA_COLS)
    col_ok = (cols[None, :] >= col_start[:, None]) & (cols[None, :] < col_start[:, None] + NA_COLS)
    dr_idx = row_idx - r_ids[:, None] + (NA_ROWS - 1)
    dc_idx = jnp.clip(cols[None, :] - cols[:, None] + (NA_COLS - 1), 0, 2 * NA_COLS - 2)
    bias = rpb[:, dr_idx[:, None, :, None], dc_idx[None, :, None, :]]
    bias = jnp.where(col_ok[None, None, :, None, :], bias.astype(F32), NEG_INF)
    return bias.reshape(rpb.shape[0], rows, GRID_W, kr * GRID_W)


def _na_lat_call(proj, cache_k, cache_v, bias, qn3, kn3, y_prev, l, B, T, row0):
    N = proj.shape[0]
    c0 = 4 * RET_HEADS
    rb = row0 // T
    rows = T // GRID_W
    kr = min(NA_ROWS, rows)
    L = cache_k.shape[3]
    return pl.pallas_call(
        functools.partial(_na_lat_kernel, rows=rows, kr=kr),
        out_shape=jax.ShapeDtypeStruct((N, NA_HEADS * NA_DH), BF16),
        grid=(B, NA_HEADS),
        in_specs=[pl.BlockSpec((T, NA_DH), lambda b, h: (rb + b, c0 + h)),
                  pl.BlockSpec((T, NA_DH), lambda b, h: (rb + b, c0 + NA_HEADS + h)),
                  pl.BlockSpec((T, NA_DH), lambda b, h: (rb + b, c0 + 2 * NA_HEADS + h)),
                  pl.BlockSpec((None, None, None, L, NA_DH), lambda b, h: (b, l, h, 0, 0)),
                  pl.BlockSpec((None, None, None, L, NA_DH), lambda b, h: (b, l, h, 0, 0)),
                  pl.BlockSpec((None, rows, GRID_W, kr * GRID_W), lambda b, h: (h, 0, 0, 0)),
                  pl.BlockSpec((None, 1, NA_DH), lambda b, h: (l, 0, 0)),
                  pl.BlockSpec((None, 1, NA_DH), lambda b, h: (l, 0, 0)),
                  pl.BlockSpec(memory_space=pl.ANY)],
        out_specs=pl.BlockSpec((T, NA_DH), lambda b, h: (rb + b, h)),
        input_output_aliases={8: 0},
        compiler_params=_cparams(("arbitrary", "arbitrary"), VMEM_LIMIT),
        name="na_latent",
    )(proj, proj, proj, cache_k, cache_v, bias, qn3, kn3, y_prev)


def _rw_prep_kernel(xr_ref, xk_ref, xv_ref, shr_ref, shk_ref, shv_ref, hl_ref, w2_ref, a2_ref, g2_ref,
                    w0_ref, a0_ref, kk_ref, ka_ref, rk_ref,
                    r_o, v_o, al_o, g_o, bon_o, lw0_o, be0_o, kd0_o, lw1_o, be1_o, kd1_o, *, T):
    row = lax.broadcasted_iota(jnp.int32, (T, 1), 0)

    def shift(x_ref, sh_ref):
        x = x_ref[...]
        prev = jnp.where(row == 0, 0.0, pltpu.roll(x, 1, 0))
        nxt = jnp.where(row == T - 1, 0.0, pltpu.roll(x, T - 1, 0))
        return sh_ref[0:1, :] * prev + sh_ref[1:2, :] * x + sh_ref[2:3, :] * nxt

    r = shift(xr_ref, shr_ref)
    k = shift(xk_ref, shk_ref)
    v = shift(xv_ref, shv_ref)
    p = _head_sum_matrix()
    hl = hl_ref[...]
    g = _dot(_sigmoid(hl[:, 4 * LANES:5 * LANES]), g2_ref[...])
    kk = k * kk_ref[...]
    kk = kk / jnp.maximum(jnp.sqrt(_seg_sum(kk * kk, p)), 1e-6)
    bonus = _seg_sum(r * k * rk_ref[...], p) * v
    r_o[...] = r
    v_o[...] = v
    al_o[...] = -kk
    g_o[...] = g
    bon_o[...] = bonus
    outs = ((lw0_o, be0_o, kd0_o), (lw1_o, be1_o, kd1_o))
    for d in range(2):
        z = w0_ref[d:d + 1, :] + _dot(jnp.tanh(hl[:, d * LANES:(d + 1) * LANES]), w2_ref[d])
        u = -z
        softplus = jnp.maximum(u, 0.0) + jnp.log(1.0 + jnp.exp(-jnp.abs(u)))
        wlog = -softplus - 0.5
        a = _sigmoid(a0_ref[d:d + 1, :] + _dot(hl[:, (2 + d) * LANES:(3 + d) * LANES], a2_ref[d]))
        lw_o, be_o, kd_o = outs[d]
        lw_o[...] = -jnp.exp(wlog)
        be_o[...] = kk * a
        kd_o[...] = k * (1.0 + (a - 1.0) * ka_ref[...])


def _rw_prep_call(proj, hl, shift3, w2p, a2p, g2, w0, a0, kk3, ka3, rk3, l, B, T, row0):
    rb = row0 // T
    nc = RW_W // RW_CB
    c0 = 7 * (RW_W // RW_CB)
    n_rows = B * T
    big = lambda off: pl.BlockSpec((T, RW_CB), lambda b, c: (rb + b, c0 + off * nc + c))
    sh = lambda off: pl.BlockSpec((None, 3, RW_CB), lambda b, c: (l, 0, off * nc + c))
    vec = pl.BlockSpec((None, 1, RW_CB), lambda b, c: (l, 0, c))
    out_spec = pl.BlockSpec((T, RW_CB), lambda b, c: (b, c))
    out = jax.ShapeDtypeStruct((n_rows, RW_W), F32)
    return pl.pallas_call(
        functools.partial(_rw_prep_kernel, T=T),
        out_shape=(out,) * 11,
        grid=(B, nc),
        in_specs=[big(0), big(1), big(2), sh(0), sh(1), sh(2),
                  pl.BlockSpec((T, LORA_W), lambda b, c: (rb + b, 0)),
                  pl.BlockSpec((None, 2, LANES, RW_CB), lambda b, c: (l, 0, 0, c)),
                  pl.BlockSpec((None, 2, LANES, RW_CB), lambda b, c: (l, 0, 0, c)),
                  pl.BlockSpec((None, LANES, RW_CB), lambda b, c: (l, 0, c)),
                  pl.BlockSpec((None, 2, RW_CB), lambda b, c: (l, 0, c)),
                  pl.BlockSpec((None, 2, RW_CB), lambda b, c: (l, 0, c)),
                  vec, vec, vec],
        out_specs=(out_spec,) * 11,
        compiler_params=_cparams(("arbitrary", "arbitrary"), VMEM_LIMIT),
        name="rwkv_prep",
    )(proj, proj, proj, shift3, shift3, shift3, hl, w2p, a2p, g2, w0, a0, kk3, ka3, rk3)


def _rw_scan_kernel(r_ref, v_ref, al_ref, lw_ref, be_ref, kd_ref, s0_ref, y_ref, sf_ref, st_ref,
                    *, reverse, nchunks):
    C = RW_CHUNK
    c = pl.program_id(1)

    @pl.when(c == 0)
    def _():
        st_ref[...] = s0_ref[...]

    ti = lax.broadcasted_iota(jnp.int32, (C, C), 0)
    si = lax.broadcasted_iota(jnp.int32, (C, C), 1)
    incl = (ti <= si) if reverse else (ti >= si)
    lw = lw_ref[...]
    cum = _dot_exact_lhs(incl.astype(F32), lw)
    tot = jnp.sum(lw, axis=0, keepdims=True)
    e_incl = jnp.exp(cum)
    e_excl = jnp.exp(cum - lw)
    e_neg = jnp.exp(-cum)
    e_rem = jnp.exp(tot - cum)
    g_tot = jnp.exp(tot)
    be = be_ref[...]
    kd = kd_ref[...]
    scaled = (al_ref[...] * e_excl, r_ref[...] * e_incl, be * e_neg, kd * e_neg, v_ref[...],
              be * e_rem, kd * e_rem)

    S = 2 * C
    t2 = lax.broadcasted_iota(jnp.int32, (S, S), 0)
    s2 = lax.broadcasted_iota(jnp.int32, (S, S), 1)
    same = (t2 // C) == (s2 // C)
    if reverse:
        strict2 = same & (t2 < s2)
        incl2 = same & (t2 <= s2)
    else:
        strict2 = same & (t2 > s2)
        incl2 = same & (t2 >= s2)
    eye = t2 == s2
    lane = lax.broadcasted_iota(jnp.int32, (C, LANES), 1)
    m0 = (lane < RW_N).astype(F32)
    m1 = 1.0 - m0

    def stack(x):
        return jnp.concatenate([x * m0, x * m1], axis=0).astype(BF16)

    pairs = range(RW_HEADS // 2)
    cat = jnp.concatenate
    sls = [slice(p * LANES, (p + 1) * LANES) for p in pairs]
    a_s, r_s, b_s, k_s, v_s, bg_s, kg_s = ([stack(x[:, sl]) for sl in sls] for x in scaled)
    hst = [st_ref[p] for p in pairs]
    h_b = [h.astype(BF16) for h in hst]
    m_all = [_dot_nt(cat([a_s[p], r_s[p]], 0), cat([b_s[p], k_s[p]], 0)) for p in pairs]
    m_ab = [jnp.where(strict2, m_all[p][:S, :S], 0.0) for p in pairs]
    m_ak = [jnp.where(strict2, m_all[p][:S, S:], 0.0).astype(BF16) for p in pairs]
    m_rb = [jnp.where(incl2, m_all[p][S:, :S], 0.0).astype(BF16) for p in pairs]
    m_rk = [jnp.where(incl2, m_all[p][S:, S:], 0.0).astype(BF16) for p in pairs]
    x = [_dot(cat([a_s[p], m_ak[p]], 1), cat([h_b[p], v_s[p]], 0)) for p in pairs]
    pw = m_ab
    n_steps = int(np.log2(C))
    for step in range(n_steps):
        if step + 1 < n_steps:
            px = [_dot(pw[p], cat([x[p], pw[p]], 1)) for p in pairs]
            x = [x[p] + px[p][:, :S] for p in pairs]
            pw = [px[p][:, S:] for p in pairs]
        else:
            x = [x[p] + _dot(pw[p], x[p]) for p in pairs]
    u_s = [x[p].astype(BF16) for p in pairs]
    y_s = [_dot(cat([r_s[p], m_rb[p], m_rk[p]], 1), cat([h_b[p], u_s[p], v_s[p]], 0)) for p in pairs]
    upd = [_dot_tn(cat([bg_s[p], kg_s[p]], 0), cat([u_s[p], v_s[p]], 0)) for p in pairs]
    for p in pairs:
        y_ref[:, sls[p]] = y_s[p][:C] + y_s[p][C:]
        g_col = jnp.sum(jnp.where(eye, jnp.broadcast_to(g_tot[:, sls[p]], (S, S)), 0.0), axis=1, keepdims=True)
        st_ref[p] = hst[p] * g_col + upd[p]

    @pl.when(c == nchunks - 1)
    def _():
        sf_ref[...] = st_ref[...]


def _rw_scan_call(r, v, al, lw, be, kd, s0, B, T, reverse):
    C = RW_CHUNK
    nch = T // C
    if reverse:
        cmap = lambda b, c: (b * nch + (nch - 1 - c), 0)
    else:
        cmap = lambda b, c: (b * nch + c, 0)
    blk = pl.BlockSpec((C, RW_W), cmap)
    st_spec = pl.BlockSpec((None, RW_HEADS // 2, LANES, LANES), lambda b, c: (b, 0, 0, 0))
    return pl.pallas_call(
        functools.partial(_rw_scan_kernel, reverse=reverse, nchunks=nch),
        out_shape=(jax.ShapeDtypeStruct((B * T, RW_W), F32),
                   jax.ShapeDtypeStruct((B, RW_HEADS // 2, LANES, LANES), F32)),
        grid=(B, nch),
        in_specs=[blk] * 6 + [st_spec],
        out_specs=(blk, st_spec),
        scratch_shapes=[pltpu.VMEM((RW_HEADS // 2, LANES, LANES), F32)],
        compiler_params=_cparams(("arbitrary", "arbitrary"), VMEM_LIMIT),
        name="rwkv_scan_bwd" if reverse else "rwkv_scan_fwd",
    )(r, v, al, lw, be, kd, s0)


def _rw_post_kernel(yf_ref, yb_ref, bon_ref, g_ref, gn_ref, *rest):
    o_ref = rest[-1]
    p = _head_sum_matrix()
    y = yf_ref[...] + yb_ref[...]
    mu = _seg_sum(y, p) * (1.0 / RW_N)
    yc = y - mu
    var = _seg_sum(yc * yc, p) * (1.0 / RW_N)
    yn = yc * lax.rsqrt(var + 64e-5) * gn_ref[...]
    o_ref[...] = ((yn + bon_ref[...]) * g_ref[...]).astype(o_ref.dtype)


def _rw_post_call(yf, yb, bonus, g, gn3, y_buf, l, row0):
    n_rows = yf.shape[0]
    tm = 256
    blk = pl.BlockSpec((tm, RW_W), lambda i: (i, 0))
    return pl.pallas_call(
        _rw_post_kernel,
        out_shape=jax.ShapeDtypeStruct(y_buf.shape, BF16),
        grid=(n_rows // tm,),
        in_specs=[blk, blk, blk, blk, pl.BlockSpec((None, 1, RW_W), lambda i: (l, 0, 0)),
                  pl.BlockSpec(memory_space=pl.ANY)],
        out_specs=pl.BlockSpec((tm, RW_W), lambda i: (row0 // tm + i, 0)),
        input_output_aliases={5: 0},
        compiler_params=_cparams(("arbitrary",), VMEM_LIMIT),
        name="rwkv_post",
    )(yf, yb, bonus, g, gn3, y_buf)


def _blockdiag_states(s):
    B = s.shape[0]
    s = jnp.swapaxes(s, -1, -2).reshape(B, RW_HEADS // 2, 2, RW_N, RW_N)
    z = jnp.zeros_like(s[:, :, 0])
    top = jnp.concatenate([s[:, :, 0], z], axis=-1)
    bot = jnp.concatenate([z, s[:, :, 1]], axis=-1)
    return jnp.concatenate([top, bot], axis=-2)


def _unblock_states(sd):
    B = sd.shape[0]
    h0 = sd[:, :, :RW_N, :RW_N]
    h1 = sd[:, :, RW_N:, RW_N:]
    return jnp.swapaxes(jnp.stack([h0, h1], axis=2).reshape(B, RW_HEADS, RW_N, RW_N), -1, -2)


def _merge_kernel(h_ref, ya_ref, yb_ref, yc_ref, wga_ref, wgb_ref, wgc_ref, bga_ref, bgb_ref, bgc_ref,
                  wa_ref, wb_ref, wc_ref, o_ref, wg_s, wbr_s):
    @pl.when(pl.program_id(1) == 0)
    def _():
        for n, (wg, wb) in enumerate(((wga_ref, wa_ref), (wgb_ref, wb_ref), (wgc_ref, wc_ref))):
            wg_s[n] = wg[...].astype(BF16)
            wbr_s[n] = wb[...].astype(BF16)

    h = h_ref[...]
    acc = None
    for n, (y_ref, bg_ref) in enumerate(((ya_ref, bga_ref), (yb_ref, bgb_ref), (yc_ref, bgc_ref))):
        gate = _sigmoid(jnp.dot(h, wg_s[n], preferred_element_type=F32) + bg_ref[...])
        term = gate * jnp.dot(y_ref[...], wbr_s[n], preferred_element_type=F32)
        acc = term if acc is None else acc + term
    o_ref[...] = acc.astype(o_ref.dtype)


def _merge_call(h, y_ret, y_na, y_rw, w_gate, b_gate3, w_ret, w_na, w_rw, l):
    N, D = h.shape
    KB = y_ret.shape[1]
    tm, tn = 256, 256
    nj = D // tn
    a_spec = pl.BlockSpec((tm, D), lambda j, i: (i, 0))
    y_spec = pl.BlockSpec((tm, KB), lambda j, i: (i, 0))
    wg = lambda n: pl.BlockSpec((None, D, tn), lambda j, i: (l, 0, n * nj + j))
    bg = lambda n: pl.BlockSpec((None, 1, tn), lambda j, i: (l, 0, n * nj + j))
    wb = pl.BlockSpec((None, KB, tn), lambda j, i: (l, 0, j))
    return pl.pallas_call(
        _merge_kernel,
        out_shape=jax.ShapeDtypeStruct((N, D), BF16),
        grid=(nj, N // tm),
        in_specs=[a_spec, y_spec, y_spec, y_spec, wg(0), wg(1), wg(2), bg(0), bg(1), bg(2), wb, wb, wb],
        out_specs=pl.BlockSpec((tm, tn), lambda j, i: (i, j)),
        scratch_shapes=[pltpu.VMEM((3, D, tn), BF16), pltpu.VMEM((3, KB, tn), BF16)],
        compiler_params=_cparams(("arbitrary", "arbitrary"), VMEM_LIMIT),
        name="gated_merge",
    )(h, y_ret, y_na, y_rw, w_gate, w_gate, w_gate, b_gate3, b_gate3, b_gate3, w_ret, w_na, w_rw)


def _out_kernel(a_ref, w_ref, x_ref, mod_ref, o_ref, wbf_ref, *, g_idx):
    @pl.when(pl.program_id(1) == 0)
    def _():
        wbf_ref[...] = w_ref[...].astype(BF16)

    m = jnp.dot(a_ref[...], wbf_ref[...], preferred_element_type=F32)
    o_ref[...] = x_ref[...] + mod_ref[g_idx:g_idx + 1, :] * m


def _out_call(a, w_out, x, mod, l, g_idx, n_prompt, t_s):
    N, D = x.shape
    tm, tn = ROW_TILE, 512
    row = _mod_row_map(tm, n_prompt, t_s)
    return pl.pallas_call(
        functools.partial(_out_kernel, g_idx=g_idx),
        out_shape=jax.ShapeDtypeStruct((N, D), F32),
        grid=(D // tn, N // tm),
        in_specs=[pl.BlockSpec((tm, D), lambda j, i: (i, 0)),
                  pl.BlockSpec((None, D, tn), lambda j, i: (l, 0, j)),
                  pl.BlockSpec((tm, tn), lambda j, i: (i, j)),
                  pl.BlockSpec((None, 6, tn), lambda j, i: (row(i), 0, j))],
        out_specs=pl.BlockSpec((tm, tn), lambda j, i: (i, j)),
        scratch_shapes=[pltpu.VMEM((D, tn), BF16)],
        compiler_params=_cparams(("arbitrary", "arbitrary"), VMEM_LIMIT),
        name="out_proj",
    )(a, w_out, x, mod)


def _router_kernel(h_ref, wt_ref, bias_ref, idx_ref, wgt_ref):
    h1, h2, h3 = _split3(h_ref[...])
    w1, w2, w3 = _split3(wt_ref[...])
    nt = lambda a, b: lax.dot_general(a, b, (((1,), (1,)), ((), ())), preferred_element_type=F32)
    logits = (nt(w1, h1) + nt(w1, h2) + nt(w2, h1) + nt(w1, h3) + nt(w2, h2) + nt(w3, h1))
    scores = _sigmoid(logits)
    sel = scores + bias_ref[...]
    row = lambda a, e: a[e:e + 1, :]
    grp = []
    for g in range(N_GROUPS):
        v = [row(sel, g * EXPERTS_PER_GROUP + j) for j in range(EXPERTS_PER_GROUP)]
        best = None
        for i in range(EXPERTS_PER_GROUP):
            for j in range(i + 1, EXPERTS_PER_GROUP):
                s = v[i] + v[j]
                best = s if best is None else jnp.maximum(best, s)
        grp.append(best)
    gbest = jnp.zeros_like(grp[0], dtype=jnp.int32)
    gval = grp[0]
    for g in range(1, N_GROUPS):
        better = grp[g] > gval
        gbest = jnp.where(better, g, gbest)
        gval = jnp.where(better, grp[g], gval)
    neg = jnp.full_like(gval, -jnp.inf)
    masked = [jnp.where(gbest == (e // EXPERTS_PER_GROUP), row(sel, e), neg) for e in range(N_EXPERTS)]

    def top(excl):
        bi = jnp.full_like(gbest, -1)
        bv = neg
        bs = jnp.zeros_like(gval)
        for e in range(N_EXPERTS):
            cand = masked[e] if excl is None else jnp.where(excl == e, neg, masked[e])
            better = cand > bv
            bi = jnp.where(better, e, bi)
            bv = jnp.where(better, cand, bv)
            bs = jnp.where(better, row(scores, e), bs)
        return bi, bs

    i1, s1 = top(None)
    i2, s2 = top(i1)
    den = s1 + s2
    idx_ref[...] = jnp.concatenate([i1, i2] + [jnp.zeros_like(i1)] * 6, axis=0)
    wgt_ref[...] = jnp.concatenate([s1 / den, s2 / den] + [jnp.zeros_like(s1)] * 6, axis=0)


def _router_call(h2, router_wt, router_bias2):
    N, D = h2.shape
    tm = 512
    return pl.pallas_call(
        _router_kernel,
        out_shape=(jax.ShapeDtypeStruct((8, N), jnp.int32), jax.ShapeDtypeStruct((8, N), F32)),
        grid=(N // tm,),
        in_specs=[pl.BlockSpec((tm, D), lambda i: (i, 0)),
                  pl.BlockSpec((N_EXPERTS, D), lambda i: (0, 0)),
                  pl.BlockSpec((N_EXPERTS, 1), lambda i: (0, 0))],
        out_specs=(pl.BlockSpec((8, tm), lambda i: (0, i)), pl.BlockSpec((8, tm), lambda i: (0, i))),
        compiler_params=_cparams(("arbitrary",), VMEM_LIMIT),
        name="moe_router",
    )(h2, router_wt, router_bias2)


def _row_copy(src_hbm, dst, sem, src_row, dst_row):
    return pltpu.make_async_copy(src_hbm.at[pl.ds(src_row, 1)], dst.at[pl.ds(dst_row, 1)], sem)


def _last_used(i, meta, n_tiles):
    return jnp.minimum(i, meta[n_tiles] - 1)


def _gather_kernel(tok_ref, meta_ref, x_hbm, o_ref, buf, sem, *, tm, n_tiles):
    i = pl.program_id(0)
    n_used = meta_ref[n_tiles]

    def issue(tile, slot):
        base = tile * tm

        def body(r, carry):
            _row_copy(x_hbm, buf.at[slot], sem.at[slot], tok_ref[base + r], r).start()
            return carry

        lax.fori_loop(0, tm, body, 0, unroll=8)

    @pl.when(i == 0)
    def _():
        issue(0, 0)

    @pl.when(i + 1 < n_used)
    def _():
        issue(i + 1, (i + 1) % 2)

    @pl.when(i < n_used)
    def _():
        slot = i % 2
        pltpu.make_async_copy(x_hbm.at[pl.ds(0, tm)], buf.at[slot], sem.at[slot]).wait()
        o_ref[...] = buf[slot].astype(o_ref.dtype)

    @pl.when(i >= n_used)
    def _():
        o_ref[...] = jnp.zeros_like(o_ref)


def _gather_call(row_tok, meta, x, n_tiles):
    D = x.shape[1]
    tm = MOE_TILE
    return pl.pallas_call(
        functools.partial(_gather_kernel, tm=tm, n_tiles=n_tiles),
        out_shape=jax.ShapeDtypeStruct((n_tiles * tm, D), BF16),
        grid_spec=pltpu.PrefetchScalarGridSpec(
            num_scalar_prefetch=2,
            grid=(n_tiles,),
            in_specs=[pl.BlockSpec(memory_space=pl.ANY)],
            out_specs=pl.BlockSpec((tm, D), lambda i, tok, meta: (i, 0)),
            scratch_shapes=[pltpu.VMEM((2, tm, D), F32), pltpu.SemaphoreType.DMA((2,))]),
        compiler_params=_cparams(("arbitrary",), VMEM_LIMIT),
        name="moe_gather",
    )(row_tok, meta, x)


def _expert_changed(meta_ref, i):
    return jnp.logical_or(i == 0, meta_ref[i] != meta_ref[jnp.maximum(i - 1, 0)])


def _expert_up_kernel(meta_ref, x_ref, wg_ref, wu_ref, o_ref, wg_s, wu_s, *, n_tiles):
    i = pl.program_id(1)

    @pl.when(_expert_changed(meta_ref, i))
    def _():
        wg_s[...] = wg_ref[...].astype(BF16)
        wu_s[...] = wu_ref[...].astype(BF16)

    @pl.when(i < meta_ref[n_tiles])
    def _():
        x = x_ref[...]
        a = jnp.dot(x, wg_s[...], preferred_element_type=F32)
        b = jnp.dot(x, wu_s[...], preferred_element_type=F32)
        o_ref[...] = (_silu(a) * b).astype(o_ref.dtype)

    @pl.when(i >= meta_ref[n_tiles])
    def _():
        o_ref[...] = jnp.zeros_like(o_ref)


def _expert_up_call(meta, xs, w_gate, w_up, l, n_tiles):
    D = xs.shape[1]
    FF = w_gate.shape[-1]
    tm, tf = MOE_TILE, 512
    w_spec = pl.BlockSpec((None, None, D, tf), lambda f, i, meta: (l, meta[i], 0, f))
    return pl.pallas_call(
        functools.partial(_expert_up_kernel, n_tiles=n_tiles),
        out_shape=jax.ShapeDtypeStruct((n_tiles * tm, FF), BF16),
        grid_spec=pltpu.PrefetchScalarGridSpec(
            num_scalar_prefetch=1,
            grid=(FF // tf, n_tiles),
            in_specs=[pl.BlockSpec((tm, D), lambda f, i, meta: (_last_used(i, meta, n_tiles), 0)),
                      w_spec, w_spec],
            out_specs=pl.BlockSpec((tm, tf), lambda f, i, meta: (i, f)),
            scratch_shapes=[pltpu.VMEM((D, tf), BF16), pltpu.VMEM((D, tf), BF16)]),
        compiler_params=_cparams(("arbitrary", "arbitrary"), VMEM_LIMIT),
        name="moe_expert_up",
    )(meta, xs, w_gate, w_up)


def _expert_down_kernel(meta_ref, h_ref, wd_ref, o_ref, wd_s, *, n_tiles):
    i = pl.program_id(1)

    @pl.when(_expert_changed(meta_ref, i))
    def _():
        wd_s[...] = wd_ref[...].astype(BF16)

    @pl.when(i < meta_ref[n_tiles])
    def _():
        o_ref[...] = jnp.dot(h_ref[...], wd_s[...], preferred_element_type=F32)

    @pl.when(i >= meta_ref[n_tiles])
    def _():
        o_ref[...] = jnp.zeros_like(o_ref)


def _expert_down_call(meta, hmid, w_down, l, n_tiles):
    FF = hmid.shape[1]
    D = w_down.shape[-1]
    tm, tn = MOE_TILE, 2048
    return pl.pallas_call(
        functools.partial(_expert_down_kernel, n_tiles=n_tiles),
        out_shape=jax.ShapeDtypeStruct((n_tiles * tm, D), F32),
        grid_spec=pltpu.PrefetchScalarGridSpec(
            num_scalar_prefetch=1,
            grid=(D // tn, n_tiles),
            in_specs=[pl.BlockSpec((tm, FF), lambda n, i, meta: (_last_used(i, meta, n_tiles), 0)),
                      pl.BlockSpec((None, None, FF, tn), lambda n, i, meta: (l, meta[i], 0, n))],
            out_specs=pl.BlockSpec((tm, tn), lambda n, i, meta: (i, n)),
            scratch_shapes=[pltpu.VMEM((FF, tn), BF16)]),
        compiler_params=_cparams(("arbitrary", "arbitrary"), VMEM_LIMIT),
        name="moe_expert_down",
    )(meta, hmid, w_down)


def _combine_kernel(pos_ref, ys_hbm, x_ref, w_ref, mod_ref, o_ref, buf, sem, *, tm, g_idx, n_tok):
    i = pl.program_id(0)

    def issue(tile, slot):
        base = tile * tm

        def body(r, carry):
            _row_copy(ys_hbm, buf.at[slot, 0], sem.at[slot, 0], pos_ref[base + r], r).start()
            _row_copy(ys_hbm, buf.at[slot, 1], sem.at[slot, 1], pos_ref[n_tok + base + r], r).start()
            return carry

        lax.fori_loop(0, tm, body, 0, unroll=8)

    @pl.when(i == 0)
    def _():
        issue(0, 0)

    @pl.when(i + 1 < n_tok // tm)
    def _():
        issue(i + 1, (i + 1) % 2)

    slot = i % 2
    for k in range(2):
        pltpu.make_async_copy(ys_hbm.at[pl.ds(0, tm)], buf.at[slot, k], sem.at[slot, k]).wait()
    w = w_ref[...]
    y = w[:, 0:1] * buf[slot, 0] + w[:, 1:2] * buf[slot, 1]
    o_ref[...] = x_ref[...] + mod_ref[g_idx:g_idx + 1, :] * y


def _combine_call(pos2, ys, x, w_cols, mod, g_idx, n_prompt, t_s):
    N, D = x.shape
    tm = MOE_TILE
    row = _mod_row_map(tm, n_prompt, t_s)
    return pl.pallas_call(
        functools.partial(_combine_kernel, tm=tm, g_idx=g_idx, n_tok=N),
        out_shape=jax.ShapeDtypeStruct((N, D), F32),
        grid_spec=pltpu.PrefetchScalarGridSpec(
            num_scalar_prefetch=1,
            grid=(N // tm,),
            in_specs=[pl.BlockSpec(memory_space=pl.ANY),
                      pl.BlockSpec((tm, D), lambda i, pos: (i, 0)),
                      pl.BlockSpec((tm, LANES), lambda i, pos: (i, 0)),
                      pl.BlockSpec((None, 6, D), lambda i, pos: (row(i), 0, 0))],
            out_specs=pl.BlockSpec((tm, D), lambda i, pos: (i, 0)),
            scratch_shapes=[pltpu.VMEM((2, 2, tm, D), F32), pltpu.SemaphoreType.DMA((2, 2))]),
        compiler_params=_cparams(("arbitrary",), VMEM_LIMIT),
        name="moe_combine",
    )(pos2, ys, x, w_cols, mod)


def _moe_plan(idx, n_tiles):
    N = idx.shape[1]
    tm = MOE_TILE
    blk = 128
    e_flat = idx.reshape(-1)
    experts = jnp.arange(N_EXPERTS, dtype=jnp.int32)
    onehot = (e_flat[:, None] == experts[None, :]).astype(F32)
    oh3 = onehot.reshape(-1, blk, N_EXPERTS)
    tri = (jnp.arange(blk)[:, None] > jnp.arange(blk)[None, :]).astype(F32)
    within = jnp.einsum("ij,bjk->bik", tri, oh3, precision=lax.Precision.HIGHEST)
    blk_tot = jnp.sum(oh3, axis=1)
    before = jnp.cumsum(blk_tot, axis=0) - blk_tot
    rank = jnp.sum((within + before[:, None, :]) * oh3, axis=-1).reshape(-1).astype(jnp.int32)
    counts = jnp.sum(blk_tot, axis=0).astype(jnp.int32)
    tiles = (counts + tm - 1) // tm
    tile_end = jnp.cumsum(tiles)
    tile_start = tile_end - tiles
    start_of = jnp.sum(onehot.astype(jnp.int32) * tile_start[None, :], axis=1)
    pos = (start_of * tm + rank).astype(jnp.int32)
    tok = jnp.tile(jnp.arange(N, dtype=jnp.int32), 2)
    row_tok = jnp.zeros((n_tiles * tm,), jnp.int32).at[pos].set(tok)
    n_used = tile_end[-1]
    t = jnp.arange(n_tiles, dtype=jnp.int32)
    tile_exp = jnp.sum((tile_end[None, :] <= t[:, None]).astype(jnp.int32), axis=1)
    e_last = jnp.max(jnp.where(tiles > 0, experts, 0))
    tile_exp = jnp.where(t < n_used, tile_exp, e_last)
    meta = jnp.concatenate([tile_exp, n_used[None]]).astype(jnp.int32)
    return row_tok, meta, pos


def _moe(h2f, x, mod, router_wt, router_bias2, w_gate, w_up, w_down, l, n_prompt, t_s):
    N = x.shape[0]
    idx8, wgt8 = _router_call(h2f, router_wt, router_bias2)
    n_tiles = (2 * N) // MOE_TILE + N_EXPERTS
    row_tok, meta, pos2 = _moe_plan(idx8[:2], n_tiles)
    xs = _gather_call(row_tok, meta, h2f, n_tiles)
    hmid = _expert_up_call(meta, xs, w_gate, w_up, l, n_tiles)
    ys = _expert_down_call(meta, hmid, w_down, l, n_tiles)
    w_cols = jnp.zeros((N, LANES), F32).at[:, 0].set(wgt8[0]).at[:, 1].set(wgt8[1])
    return _combine_call(pos2, ys, x, w_cols, mod, 5, n_prompt, t_s)


def _rope_tables(T):
    nf = RET_D // 4
    t = jnp.arange(T)
    pos = jnp.stack([t // GRID_W, t % GRID_W], -1).astype(F32)
    inv = ROPE_BASE ** (-jnp.arange(nf, dtype=F32) / nf)
    ang = pos[:, :, None] * inv
    cos, sin = jnp.cos(ang), jnp.sin(ang)
    cos_t = jnp.concatenate([cos, cos], axis=-1).reshape(T, RET_D)
    sin_t = jnp.concatenate([-sin, sin], axis=-1).reshape(T, RET_D)
    return cos_t, sin_t


def kernel(x_prompt, x_sample, state_ret_fwd, state_ret_bwd, cache_na_k, cache_na_v, state_rwkv_fwd, state_rwkv_bwd, c, c_ctx, w_mod, b_mod, norm_mix, norm_ffn, w_in, w_gate, b_gate, w_br_ret, w_br_na, w_br_rw, w_out, ret_log_decay, ret_gn, na_qn, na_kn, na_rpb, rw_shift, rw_w0, rw_w1, rw_w2, rw_a0, rw_a1, rw_a2, rw_g1, rw_g2, rw_kk, rw_ka, rw_rk, rw_gn, router_w, router_bias, moe_w_gate, moe_w_up, moe_w_down):
    BP, TP, D = x_prompt.shape
    BS, TS, _ = x_sample.shape
    depth = w_in.shape[0]
    NP, NS = BP * TP, BS * TS
    N = NP + NS
    assert NP % ROW_TILE == 0 and TS % ROW_TILE == 0 and TP % RW_CHUNK == 0 and TS % RW_CHUNK == 0
    assert TS % TP == 0 and NP % TS == 0 and BS <= 7

    x = jnp.concatenate([x_prompt.reshape(NP, D), x_sample.reshape(NS, D)], axis=0)
    c8 = jnp.concatenate([c_ctx[None, :], c, jnp.zeros((7 - BS, D), F32)], axis=0)
    b_mod3 = b_mod.reshape(depth, 1, 6 * D)
    norm_mix3 = norm_mix.reshape(depth, 1, D)
    norm_ffn3 = norm_ffn.reshape(depth, 1, D)
    b_gate3 = b_gate.reshape(depth, 1, 3 * D)
    ret_gn3 = ret_gn.reshape(depth, 1, RET_HEADS * RET_D)
    na_qn3 = na_qn.reshape(depth, 1, NA_DH)
    na_kn3 = na_kn.reshape(depth, 1, NA_DH)
    rw_kk3 = rw_kk.reshape(depth, 1, RW_W)
    rw_ka3 = rw_ka.reshape(depth, 1, RW_W)
    rw_rk3 = rw_rk.reshape(depth, 1, RW_W)
    rw_gn3 = rw_gn.reshape(depth, 1, RW_W)
    zpad = jnp.zeros((depth, D, LANES - 64), F32)
    w_lora = jnp.concatenate([rw_w1[:, 0], zpad, rw_w1[:, 1], zpad, rw_a1[:, 0], zpad, rw_a1[:, 1], zpad,
                              rw_g1], axis=-1)
    kpad = jnp.zeros((depth, 2, LANES - 64, RW_W), F32)
    w2p = jnp.concatenate([rw_w2, kpad], axis=2)
    a2p = jnp.concatenate([rw_a2, kpad], axis=2)
    router_wt = router_w.T
    router_bias2 = router_bias.reshape(N_EXPERTS, 1)
    cos_t, sin_t = _rope_tables(TS)
    zero_st = jnp.zeros((BP, RW_HEADS // 2, LANES, LANES), F32)

    ret_f = jnp.zeros((BP, depth, RET_HEADS, RET_D, RET_D), F32)
    ret_b = jnp.zeros((BP, depth, RET_HEADS, RET_D, RET_D), F32)
    na_k = jnp.zeros((BP, depth, NA_HEADS, TP, NA_DH), F32)
    na_v = jnp.zeros((BP, depth, NA_HEADS, TP, NA_DH), F32)
    rw_f, rw_b = [], []
    for l in range(depth):
        mod = _mod_call(c8, w_mod, b_mod3, l)[:1 + BS].reshape(1 + BS, 6, D)
        (h,) = _norm_mod_call(x, norm_mix3, mod, l, 0, 1, NP, TS, (BF16,))
        proj = _mm_call(h, w_in, l, 2 * ROW_TILE, 512, "in_proj")
        hl = _mm_call(h, w_lora, l, ROW_TILE, LORA_W, "lora_proj")

        y_zero = jnp.zeros((N, RW_W), BF16)
        y_ret, ret_f, ret_b = _ret_call(proj, ret_log_decay[l], ret_gn3, l, BP, TP, 0, False,
                                        (y_zero, ret_f, ret_b))
        y_ret = _ret_call(proj, ret_log_decay[l], ret_gn3, l, BS, TS, NP, True,
                          (cos_t, sin_t, state_ret_fwd, state_ret_bwd, y_ret))
        y_na, na_k, na_v = _na_ctx_call(proj, na_qn3, na_kn3, y_zero, na_k, na_v, l, BP, TP)
        bias = _na_bias(na_rpb[l], TS // GRID_W)
        y_na = _na_lat_call(proj, cache_na_k, cache_na_v, bias, na_qn3, na_kn3, y_na, l, BS, TS, NP)
        y_rw = y_zero
        for (B, T, row0, s0f, s0b) in ((BP, TP, 0, zero_st, zero_st),
                                       (BS, TS, NP, _blockdiag_states(state_rwkv_fwd[:, l]),
                                        _blockdiag_states(state_rwkv_bwd[:, l]))):
            (r, v, al, g, bonus, lw0, be0, kd0, lw1, be1, kd1) = _rw_prep_call(
                proj, hl, rw_shift, w2p, a2p, rw_g2, rw_w0, rw_a0, rw_kk3, rw_ka3, rw_rk3, l, B, T, row0)
            yf, stf = _rw_scan_call(r, v, al, lw0, be0, kd0, s0f, B, T, False)
            yb, stb = _rw_scan_call(r, v, al, lw1, be1, kd1, s0b, B, T, True)
            y_rw = _rw_post_call(yf, yb, bonus, g, rw_gn3, y_rw, l, row0)
            if row0 == 0:
                rw_f.append(_unblock_states(stf))
                rw_b.append(_unblock_states(stb))

        merged = _merge_call(h, y_ret, y_na, y_rw, w_gate, b_gate3, w_br_ret, w_br_na, w_br_rw, l)
        x = _out_call(merged, w_out, x, mod, l, 2, NP, TS)
        (h2f,) = _norm_mod_call(x, norm_ffn3, mod, l, 3, 4, NP, TS, (F32,))
        x = _moe(h2f, x, mod, router_wt, router_bias2, moe_w_gate, moe_w_up, moe_w_down, l, NP, TS)

    y_p = x[:NP].reshape(BP, TP, D)
    y_s = x[NP:].reshape(BS, TS, D)
    return (y_p, y_s, ret_f, ret_b, na_k, na_v, jnp.stack(rw_f, axis=1), jnp.stack(rw_b, axis=1))
```

```python
import functools

import jax
import jax.numpy as jnp
import numpy as np
from jax import lax
from jax.experimental import pallas as pl
from jax.experimental.pallas import tpu as pltpu

F32 = jnp.float32
BF16 = jnp.bfloat16

GRID_W = 64
RET_HEADS = 8
RET_D = 128
NA_HEADS = 8
NA_DH = 128
NA_ROWS = 8
NA_COLS = 16
RW_HEADS = 16
RW_N = 64
RW_W = RW_HEADS * RW_N
N_EXPERTS = 16
N_GROUPS = 4
EXPERTS_PER_GROUP = 4
ROPE_BASE = 10000.0
NORM_EPS = 1e-6
NEG_INF = -1e30

LANES = 128
RW_CHUNK = 64
RW_CB = 256
LORA_W = 640
ROW_TILE = 512
MOE_TILE = 256
VMEM_LIMIT = 56 * 1024 * 1024


def _cparams(sem, vmem=None):
    return pltpu.CompilerParams(dimension_semantics=sem, vmem_limit_bytes=vmem)


def _dot(a, b):
    return jnp.dot(a.astype(BF16), b.astype(BF16), preferred_element_type=F32)


def _dot_nt(a, b):
    return lax.dot_general(a.astype(BF16), b.astype(BF16), (((1,), (1,)), ((), ())),
                           preferred_element_type=F32)


def _dot_tn(a, b):
    return lax.dot_general(a.astype(BF16), b.astype(BF16), (((0,), (0,)), ((), ())),
                           preferred_element_type=F32)


def _split3(x):
    h = x.astype(BF16)
    r1 = x - h.astype(F32)
    m = r1.astype(BF16)
    l = (r1 - m.astype(F32)).astype(BF16)
    return h, m, l


def _dot_exact_lhs(c, x):
    h, m, l = _split3(x)
    cb = c.astype(BF16)
    return (jnp.dot(cb, h, preferred_element_type=F32) + jnp.dot(cb, m, preferred_element_type=F32)
            + jnp.dot(cb, l, preferred_element_type=F32))


def _dot_exact_rhs(x, c):
    h, m, l = _split3(x)
    cb = c.astype(BF16)
    return (jnp.dot(h, cb, preferred_element_type=F32) + jnp.dot(m, cb, preferred_element_type=F32)
            + jnp.dot(l, cb, preferred_element_type=F32))


def _head_sum_matrix():
    r = lax.broadcasted_iota(jnp.int32, (LANES, LANES), 0) // RW_N
    c = lax.broadcasted_iota(jnp.int32, (LANES, LANES), 1) // RW_N
    return (r == c).astype(F32)


def _seg_sum(x, p):
    parts = [_dot_exact_rhs(x[:, s * LANES:(s + 1) * LANES], p) for s in range(x.shape[1] // LANES)]
    return parts[0] if len(parts) == 1 else jnp.concatenate(parts, axis=1)


def _sigmoid(x):
    return 1.0 / (1.0 + jnp.exp(-x))


def _silu(x):
    return x * _sigmoid(x)


def _mod_kernel(c_ref, w_ref, b_ref, o_ref):
    a = _silu(c_ref[...])
    o_ref[...] = _dot(a, w_ref[...]) + b_ref[...]


def _mod_call(c8, w_mod, b_mod3, l):
    D = c8.shape[1]
    n_out = w_mod.shape[2]
    tn = 512
    return pl.pallas_call(
        _mod_kernel,
        out_shape=jax.ShapeDtypeStruct((8, n_out), F32),
        grid=(n_out // tn,),
        in_specs=[pl.BlockSpec((8, D), lambda j: (0, 0)),
                  pl.BlockSpec((None, D, tn), lambda j: (l, 0, j)),
                  pl.BlockSpec((None, 1, tn), lambda j: (l, 0, j))],
        out_specs=pl.BlockSpec((8, tn), lambda j: (0, j)),
        compiler_params=_cparams(("arbitrary",), VMEM_LIMIT),
        name="mod_proj",
    )(c8, w_mod, b_mod3)


def _norm_mod_kernel(x_ref, g_ref, mod_ref, *out_refs, sh_idx, sc_idx):
    x = x_ref[...]
    y = x * lax.rsqrt(jnp.mean(x * x, -1, keepdims=True) + NORM_EPS) * g_ref[...]
    h = y * (1.0 + mod_ref[sc_idx:sc_idx + 1, :]) + mod_ref[sh_idx:sh_idx + 1, :]
    for o in out_refs:
        o[...] = h.astype(o.dtype)


def _mod_row_map(tm, n_prompt, t_s):
    def row(i):
        r = i * tm
        return jnp.where(r < n_prompt, 0, 1 + (r - n_prompt) // t_s)
    return row


def _norm_mod_call(x, g3, mod, l, sh_idx, sc_idx, n_prompt, t_s, out_dtypes):
    N, D = x.shape
    tm = 256
    row = _mod_row_map(tm, n_prompt, t_s)
    outs = tuple(jax.ShapeDtypeStruct((N, D), dt) for dt in out_dtypes)
    return pl.pallas_call(
        functools.partial(_norm_mod_kernel, sh_idx=sh_idx, sc_idx=sc_idx),
        out_shape=outs,
        grid=(N // tm,),
        in_specs=[pl.BlockSpec((tm, D), lambda i: (i, 0)),
                  pl.BlockSpec((None, 1, D), lambda i: (l, 0, 0)),
                  pl.BlockSpec((None, 6, D), lambda i: (row(i), 0, 0))],
        out_specs=tuple(pl.BlockSpec((tm, D), lambda i: (i, 0)) for _ in out_dtypes),
        compiler_params=_cparams(("arbitrary",), VMEM_LIMIT),
        name="norm_mod",
    )(x, g3, mod)


def _mm_kernel(a_ref, w_ref, o_ref, wbf_ref):
    @pl.when(pl.program_id(1) == 0)
    def _():
        wbf_ref[...] = w_ref[...].astype(BF16)

    o_ref[...] = jnp.dot(a_ref[...], wbf_ref[...], preferred_element_type=F32).astype(o_ref.dtype)


def _mm_call(a, w, l, tm, tn, name, out_dtype=F32):
    N, K = a.shape
    M = w.shape[-1]
    if w.ndim == 3:
        w_spec = pl.BlockSpec((None, K, tn), lambda j, i: (l, 0, j))
    else:
        w_spec = pl.BlockSpec((K, tn), lambda j, i: (0, j))
    return pl.pallas_call(
        _mm_kernel,
        out_shape=jax.ShapeDtypeStruct((N, M), out_dtype),
        grid=(M // tn, N // tm),
        in_specs=[pl.BlockSpec((tm, K), lambda j, i: (i, 0)), w_spec],
        out_specs=pl.BlockSpec((tm, tn), lambda j, i: (i, j)),
        scratch_shapes=[pltpu.VMEM((K, tn), BF16)],
        compiler_params=_cparams(("arbitrary", "arbitrary"), VMEM_LIMIT),
        name=name,
    )(a, w)


def _rope(x, cos, sin):
    lane = lax.broadcasted_iota(jnp.int32, x.shape, 1)
    first = (lane % 64) < 32
    rot = jnp.where(first, pltpu.roll(x, LANES - 32, 1), pltpu.roll(x, 32, 1))
    return x * cos + rot * sin


def _ret_kernel(lg_ref, q_ref, k_ref, v_ref, g_ref, gn_ref, *rest, T, tq, latent):
    if latent:
        cosq_ref, sinq_ref, cosk_ref, sink_ref, s0f_ref, s0b_ref, _, y_ref = rest
    else:
        y_ref, sf_ref, sb_ref = rest[-3:]
    h = pl.program_id(1)
    qi = pl.program_id(2)
    lgf = lg_ref[0, h]
    lgb = lg_ref[1, h]
    q = q_ref[...]
    k = k_ref[...]
    v = v_ref[...]
    if latent:
        q = _rope(q, cosq_ref[...], sinq_ref[...])
        k = _rope(k, cosk_ref[...], sink_ref[...])
    k = k * (RET_D ** -0.5)
    t_idx = (lax.broadcasted_iota(jnp.int32, (tq, T), 0) + qi * tq).astype(F32)
    s_idx = lax.broadcasted_iota(jnp.int32, (tq, T), 1).astype(F32)
    diff = t_idx - s_idx
    dmask = (jnp.where(diff >= 0, jnp.exp(lgf * jnp.maximum(diff, 0.0)), 0.0)
             + jnp.where(diff <= 0, jnp.exp(lgb * jnp.maximum(-diff, 0.0)), 0.0))
    scores = _dot_nt(q, k) * dmask
    y = _dot(scores, v)
    if latent:
        tq_col = (lax.broadcasted_iota(jnp.int32, (tq, 1), 0) + qi * tq).astype(F32)
        y = y + _dot(q * jnp.exp(lgf * (tq_col + 1.0)), s0f_ref[...])
        y = y + _dot(q * jnp.exp(lgb * (T - tq_col)), s0b_ref[...])
    else:
        s_col = lax.broadcasted_iota(jnp.int32, (T, 1), 0).astype(F32)
        sf_ref[...] = _dot_tn(k * jnp.exp(lgf * (T - 1.0 - s_col)), v)
        sb_ref[...] = _dot_tn(k * jnp.exp(lgb * s_col), v)
    mu = jnp.mean(y, -1, keepdims=True)
    yc = y - mu
    var = jnp.mean(yc * yc, -1, keepdims=True)
    yn = yc * lax.rsqrt(var + 1e-5) * gn_ref[...]
    y_ref[...] = (yn * _silu(g_ref[...])).astype(y_ref.dtype)


def _ret_call(proj, lg, gn3, l, B, T, row0, latent, extra):
    N = proj.shape[0]
    tq = 256
    nq = T // tq
    rb = row0 // T
    qmap = lambda b, h, qi: (rb * nq + b * nq + qi, h)
    kmap = lambda b, h, qi: (rb + b, RET_HEADS + h)
    vmap = lambda b, h, qi: (rb + b, 2 * RET_HEADS + h)
    gmap = lambda b, h, qi: (rb * nq + b * nq + qi, 3 * RET_HEADS + h)
    in_specs = [pl.BlockSpec(memory_space=pltpu.SMEM),
                pl.BlockSpec((tq, RET_D), qmap),
                pl.BlockSpec((T, RET_D), kmap),
                pl.BlockSpec((T, RET_D), vmap),
                pl.BlockSpec((tq, RET_D), gmap),
                pl.BlockSpec((None, 1, RET_D), lambda b, h, qi: (l, 0, h))]
    args = [lg, proj, proj, proj, proj, gn3]
    y_spec = pl.BlockSpec((tq, RET_D), qmap)
    y_shape = jax.ShapeDtypeStruct((N, RET_HEADS * RET_D), BF16)
    if latent:
        cos, sin, s0f, s0b, y_buf = extra
        in_specs += [pl.BlockSpec((tq, RET_D), lambda b, h, qi: (qi, 0)),
                     pl.BlockSpec((tq, RET_D), lambda b, h, qi: (qi, 0)),
                     pl.BlockSpec((T, RET_D), lambda b, h, qi: (0, 0)),
                     pl.BlockSpec((T, RET_D), lambda b, h, qi: (0, 0)),
                     pl.BlockSpec((None, None, None, RET_D, RET_D), lambda b, h, qi: (b, l, h, 0, 0)),
                     pl.BlockSpec((None, None, None, RET_D, RET_D), lambda b, h, qi: (b, l, h, 0, 0)),
                     pl.BlockSpec(memory_space=pl.ANY)]
        args += [cos, sin, cos, sin, s0f, s0b, y_buf]
        aliases = {len(args) - 1: 0}
        out_shape = y_shape
        out_specs = y_spec
    else:
        y_buf, sf_buf, sb_buf = extra
        st = jax.ShapeDtypeStruct(sf_buf.shape, F32)
        st_spec = pl.BlockSpec((None, None, None, RET_D, RET_D), lambda b, h, qi: (b, l, h, 0, 0))
        in_specs += [pl.BlockSpec(memory_space=pl.ANY)] * 3
        args += [y_buf, sf_buf, sb_buf]
        aliases = {len(args) - 3: 0, len(args) - 2: 1, len(args) - 1: 2}
        out_shape = (y_shape, st, st)
        out_specs = (y_spec, st_spec, st_spec)
    return pl.pallas_call(
        functools.partial(_ret_kernel, T=T, tq=tq, latent=latent),
        out_shape=out_shape,
        grid=(B, RET_HEADS, nq),
        in_specs=in_specs,
        out_specs=out_specs,
        input_output_aliases=aliases,
        compiler_params=_cparams(("arbitrary", "arbitrary", "arbitrary"), VMEM_LIMIT),
        name="retention_latent" if latent else "retention_ctx",
    )(*args)


def _rms(x, g):
    return x * lax.rsqrt(jnp.mean(x * x, -1, keepdims=True) + NORM_EPS) * g


def _na_ctx_kernel(q_ref, k_ref, v_ref, qn_ref, kn_ref, *rest):
    y_ref, ko_ref, vo_ref = rest[-3:]
    q = _rms(q_ref[...], qn_ref[...]) * (NA_DH ** -0.5)
    k = _rms(k_ref[...], kn_ref[...])
    v = v_ref[...]
    s = _dot_nt(q, k)
    m = jnp.max(s, -1, keepdims=True)
    p = jnp.exp(s - m)
    den = jnp.sum(p, -1, keepdims=True)
    y_ref[...] = (_dot(p, v) / den).astype(y_ref.dtype)
    ko_ref[...] = k
    vo_ref[...] = v


def _na_ctx_call(proj, qn3, kn3, y_buf, k_buf, v_buf, l, B, T):
    N = proj.shape[0]
    c0 = 4 * RET_HEADS
    kv = jax.ShapeDtypeStruct(k_buf.shape, F32)
    kv_spec = pl.BlockSpec((None, None, None, T, NA_DH), lambda b, h: (b, l, h, 0, 0))
    any_spec = pl.BlockSpec(memory_space=pl.ANY)
    return pl.pallas_call(
        _na_ctx_kernel,
        out_shape=(jax.ShapeDtypeStruct((N, NA_HEADS * NA_DH), BF16), kv, kv),
        grid=(B, NA_HEADS),
        in_specs=[pl.BlockSpec((T, NA_DH), lambda b, h: (b, c0 + h)),
                  pl.BlockSpec((T, NA_DH), lambda b, h: (b, c0 + NA_HEADS + h)),
                  pl.BlockSpec((T, NA_DH), lambda b, h: (b, c0 + 2 * NA_HEADS + h)),
                  pl.BlockSpec((None, 1, NA_DH), lambda b, h: (l, 0, 0)),
                  pl.BlockSpec((None, 1, NA_DH), lambda b, h: (l, 0, 0)),
                  any_spec, any_spec, any_spec],
        out_specs=(pl.BlockSpec((T, NA_DH), lambda b, h: (b, h)), kv_spec, kv_spec),
        input_output_aliases={5: 0, 6: 1, 7: 2},
        compiler_params=_cparams(("arbitrary", "arbitrary"), VMEM_LIMIT),
        name="na_ctx",
    )(proj, proj, proj, qn3, kn3, y_buf, k_buf, v_buf)


def _na_lat_kernel(q_ref, k_ref, v_ref, ck_ref, cv_ref, bias_ref, qn_ref, kn_ref, _, y_ref, *, rows, kr):
    q = _rms(q_ref[...], qn_ref[...]) * (NA_DH ** -0.5)
    k = _rms(k_ref[...], kn_ref[...]).astype(BF16)
    v = v_ref[...].astype(BF16)
    ck = ck_ref[...].astype(BF16)
    cv = cv_ref[...].astype(BF16)
    nl = kr * GRID_W
    qb = q.astype(BF16)
    starts = [min(max(r - kr // 2, 0), rows - kr) for r in range(rows)]
    groups = []
    for r, rs in enumerate(starts):
        if groups and groups[-1][0] == rs:
            groups[-1][2] = r + 1
        else:
            groups.append([rs, r, r + 1])
    W = GRID_W
    s_ctx = _dot_nt(qb, ck)
    s_loc = [_dot_nt(qb[r0 * W:r1 * W], k[rs * W:rs * W + nl])
             + bias_ref[r0:r1].reshape((r1 - r0) * W, nl) for rs, r0, r1 in groups]
    p_loc, p_ctx, den = [], [], []
    for (rs, r0, r1), sl in zip(groups, s_loc):
        sc = s_ctx[r0 * W:r1 * W]
        m = jnp.maximum(jnp.max(sl, -1, keepdims=True), jnp.max(sc, -1, keepdims=True))
        pl_, pc_ = jnp.exp(sl - m), jnp.exp(sc - m)
        p_loc.append(pl_.astype(BF16))
        p_ctx.append(pc_.astype(BF16))
        den.append(jnp.sum(pl_, -1, keepdims=True) + jnp.sum(pc_, -1, keepdims=True))
    o_ctx = _dot(jnp.concatenate(p_ctx, axis=0), cv)
    o_loc = [_dot(p, v[rs * W:rs * W + nl]) for (rs, r0, r1), p in zip(groups, p_loc)]
    for (rs, r0, r1), ol, d in zip(groups, o_loc, den):
        y_ref[r0 * W:r1 * W, :] = ((ol + o_ctx[r0 * W:r1 * W]) / d).astype(y_ref.dtype)


def _na_bias(rpb, rows):
    H = rpb.shape[0]
    kr = min(NA_ROWS, rows)
    r_ids = np.arange(rows)
    row_start = np.clip(r_ids - kr // 2, 0, rows - kr)
    row_idx = row_start[:, None] + np.arange(kr)[None, :]
    cols = np.arange(GRID_W)
    col_start = np.clip(cols - NA_COLS // 2, 0, GRID_W - NA_COLS)
    col_ok = (cols[None, :] >= col_start[:, None]) & (cols[None, :] < col_start[:, None] + NA_COLS)
    dr_idx = row_idx - r_ids[:, None] + (NA_ROWS - 1)
    dc_idx = np.clip(cols[None, :] - cols[:, None] + (NA_COLS - 1), 0, 2 * NA_COLS - 2)
    rpb = rpb.astype(F32)
    blk = jnp.zeros((H, 2 * NA_ROWS - 1, GRID_W, GRID_W), F32)
    for d in range(2 * NA_COLS - 1):
        blk = jnp.where(jnp.asarray(dc_idx == d)[None, None], rpb[:, :, d][:, :, None, None], blk)
    blk = jnp.where(jnp.asarray(col_ok)[None, None], blk, NEG_INF)
    rows_out = [jnp.concatenate([blk[:, int(dr_idx[r, j])] for j in range(kr)], axis=-1) for r in range(rows)]
    return jnp.stack(rows_out, axis=1)


def _na_lat_call(proj, cache_k, cache_v, bias, qn3, kn3, y_prev, l, B, T, row0):
    N = proj.shape[0]
    c0 = 4 * RET_HEADS
    rb = row0 // T
    rows = T // GRID_W
    kr = min(NA_ROWS, rows)
    L = cache_k.shape[3]
    return pl.pallas_call(
        functools.partial(_na_lat_kernel, rows=rows, kr=kr),
        out_shape=jax.ShapeDtypeStruct((N, NA_HEADS * NA_DH), BF16),
        grid=(B, NA_HEADS),
        in_specs=[pl.BlockSpec((T, NA_DH), lambda b, h: (rb + b, c0 + h)),
                  pl.BlockSpec((T, NA_DH), lambda b, h: (rb + b, c0 + NA_HEADS + h)),
                  pl.BlockSpec((T, NA_DH), lambda b, h: (rb + b, c0 + 2 * NA_HEADS + h)),
                  pl.BlockSpec((None, None, None, L, NA_DH), lambda b, h: (b, l, h, 0, 0)),
                  pl.BlockSpec((None, None, None, L, NA_DH), lambda b, h: (b, l, h, 0, 0)),
                  pl.BlockSpec((None, rows, GRID_W, kr * GRID_W), lambda b, h: (h, 0, 0, 0)),
                  pl.BlockSpec((None, 1, NA_DH), lambda b, h: (l, 0, 0)),
                  pl.BlockSpec((None, 1, NA_DH), lambda b, h: (l, 0, 0)),
                  pl.BlockSpec(memory_space=pl.ANY)],
        out_specs=pl.BlockSpec((T, NA_DH), lambda b, h: (rb + b, h)),
        input_output_aliases={8: 0},
        compiler_params=_cparams(("arbitrary", "arbitrary"), VMEM_LIMIT),
        name="na_latent",
    )(proj, proj, proj, cache_k, cache_v, bias, qn3, kn3, y_prev)


def _rw_prep_kernel(xr_ref, xk_ref, xv_ref, shr_ref, shk_ref, shv_ref, hl_ref, w2_ref, a2_ref, g2_ref,
                    w0_ref, a0_ref, kk_ref, ka_ref, rk_ref,
                    r_o, v_o, al_o, g_o, bon_o, lw0_o, be0_o, kd0_o, lw1_o, be1_o, kd1_o, *, T):
    row = lax.broadcasted_iota(jnp.int32, (T, 1), 0)

    def shift(x_ref, sh_ref):
        x = x_ref[...]
        prev = jnp.where(row == 0, 0.0, pltpu.roll(x, 1, 0))
        nxt = jnp.where(row == T - 1, 0.0, pltpu.roll(x, T - 1, 0))
        return sh_ref[0:1, :] * prev + sh_ref[1:2, :] * x + sh_ref[2:3, :] * nxt

    r = shift(xr_ref, shr_ref)
    k = shift(xk_ref, shk_ref)
    v = shift(xv_ref, shv_ref)
    p = _head_sum_matrix()
    hl = hl_ref[...]
    g = _dot(_sigmoid(hl[:, 4 * LANES:5 * LANES]), g2_ref[...])
    kk = k * kk_ref[...]
    kk = kk / jnp.maximum(jnp.sqrt(_seg_sum(kk * kk, p)), 1e-6)
    bonus = _seg_sum(r * k * rk_ref[...], p) * v
    r_o[...] = r
    v_o[...] = v
    al_o[...] = -kk
    g_o[...] = g
    bon_o[...] = bonus
    outs = ((lw0_o, be0_o, kd0_o), (lw1_o, be1_o, kd1_o))
    for d in range(2):
        z = w0_ref[d:d + 1, :] + _dot(jnp.tanh(hl[:, d * LANES:(d + 1) * LANES]), w2_ref[d])
        u = -z
        softplus = jnp.maximum(u, 0.0) + jnp.log(1.0 + jnp.exp(-jnp.abs(u)))
        wlog = -softplus - 0.5
        a = _sigmoid(a0_ref[d:d + 1, :] + _dot(hl[:, (2 + d) * LANES:(3 + d) * LANES], a2_ref[d]))
        lw_o, be_o, kd_o = outs[d]
        lw_o[...] = -jnp.exp(wlog)
        be_o[...] = kk * a
        kd_o[...] = k * (1.0 + (a - 1.0) * ka_ref[...])


def _rw_prep_call(proj, hl, shift3, w2p, a2p, g2, w0, a0, kk3, ka3, rk3, l, B, T, row0):
    rb = row0 // T
    nc = RW_W // RW_CB
    c0 = 7 * (RW_W // RW_CB)
    n_rows = B * T
    big = lambda off: pl.BlockSpec((T, RW_CB), lambda b, c: (rb + b, c0 + off * nc + c))
    sh = lambda off: pl.BlockSpec((None, 3, RW_CB), lambda b, c: (l, 0, off * nc + c))
    vec = pl.BlockSpec((None, 1, RW_CB), lambda b, c: (l, 0, c))
    out_spec = pl.BlockSpec((T, RW_CB), lambda b, c: (b, c))
    out = jax.ShapeDtypeStruct((n_rows, RW_W), F32)
    return pl.pallas_call(
        functools.partial(_rw_prep_kernel, T=T),
        out_shape=(out,) * 11,
        grid=(B, nc),
        in_specs=[big(0), big(1), big(2), sh(0), sh(1), sh(2),
                  pl.BlockSpec((T, LORA_W), lambda b, c: (rb + b, 0)),
                  pl.BlockSpec((None, 2, LANES, RW_CB), lambda b, c: (l, 0, 0, c)),
                  pl.BlockSpec((None, 2, LANES, RW_CB), lambda b, c: (l, 0, 0, c)),
                  pl.BlockSpec((None, LANES, RW_CB), lambda b, c: (l, 0, c)),
                  pl.BlockSpec((None, 2, RW_CB), lambda b, c: (l, 0, c)),
                  pl.BlockSpec((None, 2, RW_CB), lambda b, c: (l, 0, c)),
                  vec, vec, vec],
        out_specs=(out_spec,) * 11,
        compiler_params=_cparams(("arbitrary", "arbitrary"), VMEM_LIMIT),
        name="rwkv_prep",
    )(proj, proj, proj, shift3, shift3, shift3, hl, w2p, a2p, g2, w0, a0, kk3, ka3, rk3)


def _rw_scan_kernel(r_ref, v_ref, al_ref, lw_ref, be_ref, kd_ref, s0_ref, y_ref, sf_ref, st_ref,
                    *, reverse, nchunks):
    C = RW_CHUNK
    c = pl.program_id(1)

    @pl.when(c == 0)
    def _():
        st_ref[...] = s0_ref[...]

    ti = lax.broadcasted_iota(jnp.int32, (C, C), 0)
    si = lax.broadcasted_iota(jnp.int32, (C, C), 1)
    incl = (ti <= si) if reverse else (ti >= si)
    lw = lw_ref[...]
    cum = _dot_exact_lhs(incl.astype(F32), lw)
    tot = jnp.sum(lw, axis=0, keepdims=True)
    e_incl = jnp.exp(cum)
    e_excl = jnp.exp(cum - lw)
    e_neg = jnp.exp(-cum)
    e_rem = jnp.exp(tot - cum)
    g_tot = jnp.exp(tot)
    be = be_ref[...]
    kd = kd_ref[...]
    scaled = (al_ref[...] * e_excl, r_ref[...] * e_incl, be * e_neg, kd * e_neg, v_ref[...],
              be * e_rem, kd * e_rem)

    S = 2 * C
    t2 = lax.broadcasted_iota(jnp.int32, (S, S), 0)
    s2 = lax.broadcasted_iota(jnp.int32, (S, S), 1)
    same = (t2 // C) == (s2 // C)
    if reverse:
        strict2 = same & (t2 < s2)
        incl2 = same & (t2 <= s2)
    else:
        strict2 = same & (t2 > s2)
        incl2 = same & (t2 >= s2)
    eye = t2 == s2
    lane = lax.broadcasted_iota(jnp.int32, (C, LANES), 1)
    m0 = (lane < RW_N).astype(F32)
    m1 = 1.0 - m0

    def stack(x):
        return jnp.concatenate([x * m0, x * m1], axis=0).astype(BF16)

    pairs = range(RW_HEADS // 2)
    cat = jnp.concatenate
    sls = [slice(p * LANES, (p + 1) * LANES) for p in pairs]
    a_s, r_s, b_s, k_s, v_s, bg_s, kg_s = ([stack(x[:, sl]) for sl in sls] for x in scaled)
    hst = [st_ref[p] for p in pairs]
    h_b = [h.astype(BF16) for h in hst]
    m_all = [_dot_nt(cat([a_s[p], r_s[p]], 0), cat([b_s[p], k_s[p]], 0)) for p in pairs]
    m_ab = [jnp.where(strict2, m_all[p][:S, :S], 0.0) for p in pairs]
    m_ak = [jnp.where(strict2, m_all[p][:S, S:], 0.0).astype(BF16) for p in pairs]
    m_rb = [jnp.where(incl2, m_all[p][S:, :S], 0.0).astype(BF16) for p in pairs]
    m_rk = [jnp.where(incl2, m_all[p][S:, S:], 0.0).astype(BF16) for p in pairs]
    x = [_dot(cat([a_s[p], m_ak[p]], 1), cat([h_b[p], v_s[p]], 0)) for p in pairs]
    pw = m_ab
    n_steps = int(np.log2(C))
    for step in range(n_steps):
        if step + 1 < n_steps:
            px = [_dot(pw[p], cat([x[p], pw[p]], 1)) for p in pairs]
            x = [x[p] + px[p][:, :S] for p in pairs]
            pw = [px[p][:, S:] for p in pairs]
        else:
            x = [x[p] + _dot(pw[p], x[p]) for p in pairs]
    u_s = [x[p].astype(BF16) for p in pairs]
    y_s = [_dot(cat([r_s[p], m_rb[p], m_rk[p]], 1), cat([h_b[p], u_s[p], v_s[p]], 0)) for p in pairs]
    upd = [_dot_tn(cat([bg_s[p], kg_s[p]], 0), cat([u_s[p], v_s[p]], 0)) for p in pairs]
    for p in pairs:
        y_ref[:, sls[p]] = y_s[p][:C] + y_s[p][C:]
        g_col = jnp.sum(jnp.where(eye, jnp.broadcast_to(g_tot[:, sls[p]], (S, S)), 0.0), axis=1, keepdims=True)
        st_ref[p] = hst[p] * g_col + upd[p]

    @pl.when(c == nchunks - 1)
    def _():
        sf_ref[...] = st_ref[...]


def _rw_scan_call(r, v, al, lw, be, kd, s0, B, T, reverse):
    C = RW_CHUNK
    nch = T // C
    if reverse:
        cmap = lambda b, c: (b * nch + (nch - 1 - c), 0)
    else:
        cmap = lambda b, c: (b * nch + c, 0)
    blk = pl.BlockSpec((C, RW_W), cmap)
    st_spec = pl.BlockSpec((None, RW_HEADS // 2, LANES, LANES), lambda b, c: (b, 0, 0, 0))
    return pl.pallas_call(
        functools.partial(_rw_scan_kernel, reverse=reverse, nchunks=nch),
        out_shape=(jax.ShapeDtypeStruct((B * T, RW_W), F32),
                   jax.ShapeDtypeStruct((B, RW_HEADS // 2, LANES, LANES), F32)),
        grid=(B, nch),
        in_specs=[blk] * 6 + [st_spec],
        out_specs=(blk, st_spec),
        scratch_shapes=[pltpu.VMEM((RW_HEADS // 2, LANES, LANES), F32)],
        compiler_params=_cparams(("arbitrary", "arbitrary"), VMEM_LIMIT),
        name="rwkv_scan_bwd" if reverse else "rwkv_scan_fwd",
    )(r, v, al, lw, be, kd, s0)


def _rw_post_kernel(yf_ref, yb_ref, bon_ref, g_ref, gn_ref, *rest):
    o_ref = rest[-1]
    p = _head_sum_matrix()
    y = yf_ref[...] + yb_ref[...]
    mu = _seg_sum(y, p) * (1.0 / RW_N)
    yc = y - mu
    var = _seg_sum(yc * yc, p) * (1.0 / RW_N)
    yn = yc * lax.rsqrt(var + 64e-5) * gn_ref[...]
    o_ref[...] = ((yn + bon_ref[...]) * g_ref[...]).astype(o_ref.dtype)


def _rw_post_call(yf, yb, bonus, g, gn3, y_buf, l, row0):
    n_rows = yf.shape[0]
    tm = 256
    blk = pl.BlockSpec((tm, RW_W), lambda i: (i, 0))
    return pl.pallas_call(
        _rw_post_kernel,
        out_shape=jax.ShapeDtypeStruct(y_buf.shape, BF16),
        grid=(n_rows // tm,),
        in_specs=[blk, blk, blk, blk, pl.BlockSpec((None, 1, RW_W), lambda i: (l, 0, 0)),
                  pl.BlockSpec(memory_space=pl.ANY)],
        out_specs=pl.BlockSpec((tm, RW_W), lambda i: (row0 // tm + i, 0)),
        input_output_aliases={5: 0},
        compiler_params=_cparams(("arbitrary",), VMEM_LIMIT),
        name="rwkv_post",
    )(yf, yb, bonus, g, gn3, y_buf)


def _blockdiag_states(s):
    B = s.shape[0]
    s = jnp.swapaxes(s, -1, -2).reshape(B, RW_HEADS // 2, 2, RW_N, RW_N)
    z = jnp.zeros_like(s[:, :, 0])
    top = jnp.concatenate([s[:, :, 0], z], axis=-1)
    bot = jnp.concatenate([z, s[:, :, 1]], axis=-1)
    return jnp.concatenate([top, bot], axis=-2)


def _unblock_states(sd):
    B = sd.shape[0]
    h0 = sd[:, :, :RW_N, :RW_N]
    h1 = sd[:, :, RW_N:, RW_N:]
    return jnp.swapaxes(jnp.stack([h0, h1], axis=2).reshape(B, RW_HEADS, RW_N, RW_N), -1, -2)


def _merge_kernel(h_ref, ya_ref, yb_ref, yc_ref, wga_ref, wgb_ref, wgc_ref, bga_ref, bgb_ref, bgc_ref,
                  wa_ref, wb_ref, wc_ref, o_ref, wg_s, wbr_s):
    @pl.when(pl.program_id(1) == 0)
    def _():
        for n, (wg, wb) in enumerate(((wga_ref, wa_ref), (wgb_ref, wb_ref), (wgc_ref, wc_ref))):
            wg_s[n] = wg[...].astype(BF16)
            wbr_s[n] = wb[...].astype(BF16)

    h = h_ref[...]
    acc = None
    for n, (y_ref, bg_ref) in enumerate(((ya_ref, bga_ref), (yb_ref, bgb_ref), (yc_ref, bgc_ref))):
        gate = _sigmoid(jnp.dot(h, wg_s[n], preferred_element_type=F32) + bg_ref[...])
        term = gate * jnp.dot(y_ref[...], wbr_s[n], preferred_element_type=F32)
        acc = term if acc is None else acc + term
    o_ref[...] = acc.astype(o_ref.dtype)


def _merge_call(h, y_ret, y_na, y_rw, w_gate, b_gate3, w_ret, w_na, w_rw, l):
    N, D = h.shape
    KB = y_ret.shape[1]
    tm, tn = ROW_TILE, 256
    nj = D // tn
    a_spec = pl.BlockSpec((tm, D), lambda j, i: (i, 0))
    y_spec = pl.BlockSpec((tm, KB), lambda j, i: (i, 0))
    once = pl.Buffered(1)
    wg = lambda n: pl.BlockSpec((None, D, tn), lambda j, i: (l, 0, n * nj + j), pipeline_mode=once)
    bg = lambda n: pl.BlockSpec((None, 1, tn), lambda j, i: (l, 0, n * nj + j))
    wb = pl.BlockSpec((None, KB, tn), lambda j, i: (l, 0, j), pipeline_mode=once)
    return pl.pallas_call(
        _merge_kernel,
        out_shape=jax.ShapeDtypeStruct((N, D), BF16),
        grid=(nj, N // tm),
        in_specs=[a_spec, y_spec, y_spec, y_spec, wg(0), wg(1), wg(2), bg(0), bg(1), bg(2), wb, wb, wb],
        out_specs=pl.BlockSpec((tm, tn), lambda j, i: (i, j)),
        scratch_shapes=[pltpu.VMEM((3, D, tn), BF16), pltpu.VMEM((3, KB, tn), BF16)],
        compiler_params=_cparams(("arbitrary", "arbitrary"), VMEM_LIMIT),
        name="gated_merge",
    )(h, y_ret, y_na, y_rw, w_gate, w_gate, w_gate, b_gate3, b_gate3, b_gate3, w_ret, w_na, w_rw)


def _out_kernel(a_ref, w_ref, x_ref, mod_ref, o_ref, wbf_ref, *, g_idx):
    @pl.when(pl.program_id(1) == 0)
    def _():
        wbf_ref[...] = w_ref[...].astype(BF16)

    m = jnp.dot(a_ref[...], wbf_ref[...], preferred_element_type=F32)
    o_ref[...] = x_ref[...] + mod_ref[g_idx:g_idx + 1, :] * m


def _out_call(a, w_out, x, mod, l, g_idx, n_prompt, t_s):
    N, D = x.shape
    tm, tn = 2 * ROW_TILE, 512
    row = _mod_row_map(tm, n_prompt, t_s)
    return pl.pallas_call(
        functools.partial(_out_kernel, g_idx=g_idx),
        out_shape=jax.ShapeDtypeStruct((N, D), F32),
        grid=(D // tn, N // tm),
        in_specs=[pl.BlockSpec((tm, D), lambda j, i: (i, 0)),
                  pl.BlockSpec((None, D, tn), lambda j, i: (l, 0, j)),
                  pl.BlockSpec((tm, tn), lambda j, i: (i, j)),
                  pl.BlockSpec((None, 6, tn), lambda j, i: (row(i), 0, j))],
        out_specs=pl.BlockSpec((tm, tn), lambda j, i: (i, j)),
        scratch_shapes=[pltpu.VMEM((D, tn), BF16)],
        compiler_params=_cparams(("arbitrary", "arbitrary"), VMEM_LIMIT),
        name="out_proj",
    )(a, w_out, x, mod)


def _router_kernel(h_ref, wt_ref, bias_ref, idx_ref, wgt_ref):
    h1, h2, h3 = _split3(h_ref[...])
    w1, w2, w3 = _split3(wt_ref[...])
    nt = lambda a, b: lax.dot_general(a, b, (((1,), (1,)), ((), ())), preferred_element_type=F32)
    logits = (nt(w1, h1) + nt(w1, h2) + nt(w2, h1) + nt(w1, h3) + nt(w2, h2) + nt(w3, h1))
    scores = _sigmoid(logits)
    sel = scores + bias_ref[...]
    row = lambda a, e: a[e:e + 1, :]
    grp = []
    for g in range(N_GROUPS):
        v = [row(sel, g * EXPERTS_PER_GROUP + j) for j in range(EXPERTS_PER_GROUP)]
        best = None
        for i in range(EXPERTS_PER_GROUP):
            for j in range(i + 1, EXPERTS_PER_GROUP):
                s = v[i] + v[j]
                best = s if best is None else jnp.maximum(best, s)
        grp.append(best)
    gbest = jnp.zeros_like(grp[0], dtype=jnp.int32)
    gval = grp[0]
    for g in range(1, N_GROUPS):
        better = grp[g] > gval
        gbest = jnp.where(better, g, gbest)
        gval = jnp.where(better, grp[g], gval)
    neg = jnp.full_like(gval, -jnp.inf)
    masked = [jnp.where(gbest == (e // EXPERTS_PER_GROUP), row(sel, e), neg) for e in range(N_EXPERTS)]

    def top(excl):
        bi = jnp.full_like(gbest, -1)
        bv = neg
        bs = jnp.zeros_like(gval)
        for e in range(N_EXPERTS):
            cand = masked[e] if excl is None else jnp.where(excl == e, neg, masked[e])
            better = cand > bv
            bi = jnp.where(better, e, bi)
            bv = jnp.where(better, cand, bv)
            bs = jnp.where(better, row(scores, e), bs)
        return bi, bs

    i1, s1 = top(None)
    i2, s2 = top(i1)
    den = s1 + s2
    idx_ref[...] = jnp.concatenate([i1, i2] + [jnp.zeros_like(i1)] * 6, axis=0)
    wgt_ref[...] = jnp.concatenate([s1 / den, s2 / den] + [jnp.zeros_like(s1)] * 6, axis=0)


def _router_call(h2, router_wt, router_bias2):
    N, D = h2.shape
    tm = 512
    return pl.pallas_call(
        _router_kernel,
        out_shape=(jax.ShapeDtypeStruct((8, N), jnp.int32), jax.ShapeDtypeStruct((8, N), F32)),
        grid=(N // tm,),
        in_specs=[pl.BlockSpec((tm, D), lambda i: (i, 0)),
                  pl.BlockSpec((N_EXPERTS, D), lambda i: (0, 0)),
                  pl.BlockSpec((N_EXPERTS, 1), lambda i: (0, 0))],
        out_specs=(pl.BlockSpec((8, tm), lambda i: (0, i)), pl.BlockSpec((8, tm), lambda i: (0, i))),
        compiler_params=_cparams(("arbitrary",), VMEM_LIMIT),
        name="moe_router",
    )(h2, router_wt, router_bias2)


def _row_copy(src_hbm, dst, sem, src_row, dst_row):
    return pltpu.make_async_copy(src_hbm.at[pl.ds(src_row, 1)], dst.at[pl.ds(dst_row, 1)], sem)


def _last_used(i, meta, n_tiles):
    return jnp.minimum(i, meta[n_tiles] - 1)


def _gather_kernel(tok_ref, meta_ref, x_hbm, o_ref, buf, sem, *, tm, n_tiles):
    i = pl.program_id(0)
    n_used = meta_ref[n_tiles]

    def issue(tile, slot):
        base = tile * tm

        def body(r, carry):
            _row_copy(x_hbm, buf.at[slot], sem.at[slot], tok_ref[base + r], r).start()
            return carry

        lax.fori_loop(0, tm, body, 0, unroll=8)

    @pl.when(i == 0)
    def _():
        issue(0, 0)

    @pl.when(i + 1 < n_used)
    def _():
        issue(i + 1, (i + 1) % 2)

    @pl.when(i < n_used)
    def _():
        slot = i % 2
        pltpu.make_async_copy(x_hbm.at[pl.ds(0, tm)], buf.at[slot], sem.at[slot]).wait()
        o_ref[...] = buf[slot].astype(o_ref.dtype)

    @pl.when(i >= n_used)
    def _():
        o_ref[...] = jnp.zeros_like(o_ref)


def _gather_call(row_tok, meta, x, n_tiles):
    D = x.shape[1]
    tm = MOE_TILE
    return pl.pallas_call(
        functools.partial(_gather_kernel, tm=tm, n_tiles=n_tiles),
        out_shape=jax.ShapeDtypeStruct((n_tiles * tm, D), BF16),
        grid_spec=pltpu.PrefetchScalarGridSpec(
            num_scalar_prefetch=2,
            grid=(n_tiles,),
            in_specs=[pl.BlockSpec(memory_space=pl.ANY)],
            out_specs=pl.BlockSpec((tm, D), lambda i, tok, meta: (i, 0)),
            scratch_shapes=[pltpu.VMEM((2, tm, D), F32), pltpu.SemaphoreType.DMA((2,))]),
        compiler_params=_cparams(("arbitrary",), VMEM_LIMIT),
        name="moe_gather",
    )(row_tok, meta, x)


def _expert_changed(meta_ref, i):
    return jnp.logical_or(i == 0, meta_ref[i] != meta_ref[jnp.maximum(i - 1, 0)])


def _expert_up_kernel(meta_ref, x_ref, wg_ref, wu_ref, o_ref, wg_s, wu_s, *, n_tiles):
    i = pl.program_id(1)

    @pl.when(_expert_changed(meta_ref, i))
    def _():
        wg_s[...] = wg_ref[...].astype(BF16)
        wu_s[...] = wu_ref[...].astype(BF16)

    @pl.when(i < meta_ref[n_tiles])
    def _():
        x = x_ref[...]
        a = jnp.dot(x, wg_s[...], preferred_element_type=F32)
        b = jnp.dot(x, wu_s[...], preferred_element_type=F32)
        o_ref[...] = (_silu(a) * b).astype(o_ref.dtype)

    @pl.when(i >= meta_ref[n_tiles])
    def _():
        o_ref[...] = jnp.zeros_like(o_ref)


def _expert_up_call(meta, xs, w_gate, w_up, l, n_tiles):
    D = xs.shape[1]
    FF = w_gate.shape[-1]
    tm, tf = MOE_TILE, 512
    w_spec = pl.BlockSpec((None, None, D, tf), lambda f, i, meta: (l, meta[i], 0, f))
    return pl.pallas_call(
        functools.partial(_expert_up_kernel, n_tiles=n_tiles),
        out_shape=jax.ShapeDtypeStruct((n_tiles * tm, FF), BF16),
        grid_spec=pltpu.PrefetchScalarGridSpec(
            num_scalar_prefetch=1,
            grid=(FF // tf, n_tiles),
            in_specs=[pl.BlockSpec((tm, D), lambda f, i, meta: (_last_used(i, meta, n_tiles), 0)),
                      w_spec, w_spec],
            out_specs=pl.BlockSpec((tm, tf), lambda f, i, meta: (i, f)),
            scratch_shapes=[pltpu.VMEM((D, tf), BF16), pltpu.VMEM((D, tf), BF16)]),
        compiler_params=_cparams(("arbitrary", "arbitrary"), VMEM_LIMIT),
        name="moe_expert_up",
    )(meta, xs, w_gate, w_up)


def _expert_down_kernel(meta_ref, h_ref, wd_ref, o_ref, wd_s, *, n_tiles):
    i = pl.program_id(1)

    @pl.when(_expert_changed(meta_ref, i))
    def _():
        wd_s[...] = wd_ref[...].astype(BF16)

    @pl.when(i < meta_ref[n_tiles])
    def _():
        o_ref[...] = jnp.dot(h_ref[...], wd_s[...], preferred_element_type=F32)

    @pl.when(i >= meta_ref[n_tiles])
    def _():
        o_ref[...] = jnp.zeros_like(o_ref)


def _expert_down_call(meta, hmid, w_down, l, n_tiles):
    FF = hmid.shape[1]
    D = w_down.shape[-1]
    tm, tn = MOE_TILE, 2048
    return pl.pallas_call(
        functools.partial(_expert_down_kernel, n_tiles=n_tiles),
        out_shape=jax.ShapeDtypeStruct((n_tiles * tm, D), F32),
        grid_spec=pltpu.PrefetchScalarGridSpec(
            num_scalar_prefetch=1,
            grid=(D // tn, n_tiles),
            in_specs=[pl.BlockSpec((tm, FF), lambda n, i, meta: (_last_used(i, meta, n_tiles), 0)),
                      pl.BlockSpec((None, None, FF, tn), lambda n, i, meta: (l, meta[i], 0, n))],
            out_specs=pl.BlockSpec((tm, tn), lambda n, i, meta: (i, n)),
            scratch_shapes=[pltpu.VMEM((FF, tn), BF16)]),
        compiler_params=_cparams(("arbitrary", "arbitrary"), VMEM_LIMIT),
        name="moe_expert_down",
    )(meta, hmid, w_down)


def _combine_kernel(pos_ref, ys_hbm, x_ref, w_ref, mod_ref, o_ref, buf, sem, *, tm, g_idx, n_tok):
    i = pl.program_id(0)

    def issue(tile, slot):
        base = tile * tm

        def body(r, carry):
            _row_copy(ys_hbm, buf.at[slot, 0], sem.at[slot, 0], pos_ref[base + r], r).start()
            _row_copy(ys_hbm, buf.at[slot, 1], sem.at[slot, 1], pos_ref[n_tok + base + r], r).start()
            return carry

        lax.fori_loop(0, tm, body, 0, unroll=8)

    @pl.when(i == 0)
    def _():
        issue(0, 0)

    @pl.when(i + 1 < n_tok // tm)
    def _():
        issue(i + 1, (i + 1) % 2)

    slot = i % 2
    for k in range(2):
        pltpu.make_async_copy(ys_hbm.at[pl.ds(0, tm)], buf.at[slot, k], sem.at[slot, k]).wait()
    w = w_ref[...]
    y = w[:, 0:1] * buf[slot, 0] + w[:, 1:2] * buf[slot, 1]
    o_ref[...] = x_ref[...] + mod_ref[g_idx:g_idx + 1, :] * y


def _combine_call(pos2, ys, x, w_cols, mod, g_idx, n_prompt, t_s):
    N, D = x.shape
    tm = MOE_TILE
    row = _mod_row_map(tm, n_prompt, t_s)
    return pl.pallas_call(
        functools.partial(_combine_kernel, tm=tm, g_idx=g_idx, n_tok=N),
        out_shape=jax.ShapeDtypeStruct((N, D), F32),
        grid_spec=pltpu.PrefetchScalarGridSpec(
            num_scalar_prefetch=1,
            grid=(N // tm,),
            in_specs=[pl.BlockSpec(memory_space=pl.ANY),
                      pl.BlockSpec((tm, D), lambda i, pos: (i, 0)),
                      pl.BlockSpec((tm, LANES), lambda i, pos: (i, 0)),
                      pl.BlockSpec((None, 6, D), lambda i, pos: (row(i), 0, 0))],
            out_specs=pl.BlockSpec((tm, D), lambda i, pos: (i, 0)),
            scratch_shapes=[pltpu.VMEM((2, 2, tm, D), F32), pltpu.SemaphoreType.DMA((2, 2))]),
        compiler_params=_cparams(("arbitrary",), VMEM_LIMIT),
        name="moe_combine",
    )(pos2, ys, x, w_cols, mod)


def _moe_plan(idx, n_tiles):
    N = idx.shape[1]
    tm = MOE_TILE
    blk = 128
    e_flat = idx.reshape(-1)
    experts = jnp.arange(N_EXPERTS, dtype=jnp.int32)
    onehot = (e_flat[:, None] == experts[None, :]).astype(F32)
    oh3 = onehot.reshape(-1, blk, N_EXPERTS)
    tri = (jnp.arange(blk)[:, None] > jnp.arange(blk)[None, :]).astype(F32)
    within = jnp.einsum("ij,bjk->bik", tri, oh3, precision=lax.Precision.HIGHEST)
    blk_tot = jnp.sum(oh3, axis=1)
    before = jnp.cumsum(blk_tot, axis=0) - blk_tot
    rank = jnp.sum((within + before[:, None, :]) * oh3, axis=-1).reshape(-1).astype(jnp.int32)
    counts = jnp.sum(blk_tot, axis=0).astype(jnp.int32)
    tiles = (counts + tm - 1) // tm
    tile_end = jnp.cumsum(tiles)
    tile_start = tile_end - tiles
    start_of = jnp.sum(onehot.astype(jnp.int32) * tile_start[None, :], axis=1)
    pos = (start_of * tm + rank).astype(jnp.int32)
    tok = jnp.tile(jnp.arange(N, dtype=jnp.int32), 2)
    row_tok = jnp.zeros((n_tiles * tm,), jnp.int32).at[pos].set(tok)
    n_used = tile_end[-1]
    t = jnp.arange(n_tiles, dtype=jnp.int32)
    tile_exp = jnp.sum((tile_end[None, :] <= t[:, None]).astype(jnp.int32), axis=1)
    e_last = jnp.max(jnp.where(tiles > 0, experts, 0))
    tile_exp = jnp.where(t < n_used, tile_exp, e_last)
    meta = jnp.concatenate([tile_exp, n_used[None]]).astype(jnp.int32)
    return row_tok, meta, pos


def _moe(h2f, x, mod, router_wt, router_bias2, w_gate, w_up, w_down, l, n_prompt, t_s):
    N = x.shape[0]
    idx8, wgt8 = _router_call(h2f, router_wt, router_bias2)
    n_tiles = (2 * N) // MOE_TILE + N_EXPERTS
    row_tok, meta, pos2 = _moe_plan(idx8[:2], n_tiles)
    xs = _gather_call(row_tok, meta, h2f, n_tiles)
    hmid = _expert_up_call(meta, xs, w_gate, w_up, l, n_tiles)
    ys = _expert_down_call(meta, hmid, w_down, l, n_tiles)
    w_cols = jnp.zeros((N, LANES), F32).at[:, 0].set(wgt8[0]).at[:, 1].set(wgt8[1])
    return _combine_call(pos2, ys, x, w_cols, mod, 5, n_prompt, t_s)


def _rope_tables(T):
    nf = RET_D // 4
    t = jnp.arange(T)
    pos = jnp.stack([t // GRID_W, t % GRID_W], -1).astype(F32)
    inv = ROPE_BASE ** (-jnp.arange(nf, dtype=F32) / nf)
    ang = pos[:, :, None] * inv
    cos, sin = jnp.cos(ang), jnp.sin(ang)
    cos_t = jnp.concatenate([cos, cos], axis=-1).reshape(T, RET_D)
    sin_t = jnp.concatenate([-sin, sin], axis=-1).reshape(T, RET_D)
    return cos_t, sin_t


def kernel(x_prompt, x_sample, state_ret_fwd, state_ret_bwd, cache_na_k, cache_na_v, state_rwkv_fwd, state_rwkv_bwd, c, c_ctx, w_mod, b_mod, norm_mix, norm_ffn, w_in, w_gate, b_gate, w_br_ret, w_br_na, w_br_rw, w_out, ret_log_decay, ret_gn, na_qn, na_kn, na_rpb, rw_shift, rw_w0, rw_w1, rw_w2, rw_a0, rw_a1, rw_a2, rw_g1, rw_g2, rw_kk, rw_ka, rw_rk, rw_gn, router_w, router_bias, moe_w_gate, moe_w_up, moe_w_down):
    BP, TP, D = x_prompt.shape
    BS, TS, _ = x_sample.shape
    depth = w_in.shape[0]
    NP, NS = BP * TP, BS * TS
    N = NP + NS
    assert NP % (2 * ROW_TILE) == 0 and TS % (2 * ROW_TILE) == 0 and TP % RW_CHUNK == 0 and TS % RW_CHUNK == 0
    assert TS % TP == 0 and NP % TS == 0 and BS <= 7

    x = jnp.concatenate([x_prompt.reshape(NP, D), x_sample.reshape(NS, D)], axis=0)
    c8 = jnp.concatenate([c_ctx[None, :], c, jnp.zeros((7 - BS, D), F32)], axis=0)
    b_mod3 = b_mod.reshape(depth, 1, 6 * D)
    norm_mix3 = norm_mix.reshape(depth, 1, D)
    norm_ffn3 = norm_ffn.reshape(depth, 1, D)
    b_gate3 = b_gate.reshape(depth, 1, 3 * D)
    ret_gn3 = ret_gn.reshape(depth, 1, RET_HEADS * RET_D)
    na_qn3 = na_qn.reshape(depth, 1, NA_DH)
    na_kn3 = na_kn.reshape(depth, 1, NA_DH)
    rw_kk3 = rw_kk.reshape(depth, 1, RW_W)
    rw_ka3 = rw_ka.reshape(depth, 1, RW_W)
    rw_rk3 = rw_rk.reshape(depth, 1, RW_W)
    rw_gn3 = rw_gn.reshape(depth, 1, RW_W)
    zpad = jnp.zeros((depth, D, LANES - 64), F32)
    w_lora = jnp.concatenate([rw_w1[:, 0], zpad, rw_w1[:, 1], zpad, rw_a1[:, 0], zpad, rw_a1[:, 1], zpad,
                              rw_g1], axis=-1)
    kpad = jnp.zeros((depth, 2, LANES - 64, RW_W), F32)
    w2p = jnp.concatenate([rw_w2, kpad], axis=2)
    a2p = jnp.concatenate([rw_a2, kpad], axis=2)
    router_wt = router_w.T
    router_bias2 = router_bias.reshape(N_EXPERTS, 1)
    cos_t, sin_t = _rope_tables(TS)
    zero_st = jnp.zeros((BP, RW_HEADS // 2, LANES, LANES), F32)

    ret_f = jnp.zeros((BP, depth, RET_HEADS, RET_D, RET_D), F32)
    ret_b = jnp.zeros((BP, depth, RET_HEADS, RET_D, RET_D), F32)
    na_k = jnp.zeros((BP, depth, NA_HEADS, TP, NA_DH), F32)
    na_v = jnp.zeros((BP, depth, NA_HEADS, TP, NA_DH), F32)
    rw_f, rw_b = [], []
    for l in range(depth):
        mod = _mod_call(c8, w_mod, b_mod3, l)[:1 + BS].reshape(1 + BS, 6, D)
        (h,) = _norm_mod_call(x, norm_mix3, mod, l, 0, 1, NP, TS, (BF16,))
        proj = _mm_call(h, w_in, l, 2 * ROW_TILE, 512, "in_proj")
        hl = _mm_call(h, w_lora, l, ROW_TILE, LORA_W, "lora_proj")

        y_zero = jnp.zeros((N, RW_W), BF16)
        y_ret, ret_f, ret_b = _ret_call(proj, ret_log_decay[l], ret_gn3, l, BP, TP, 0, False,
                                        (y_zero, ret_f, ret_b))
        y_ret = _ret_call(proj, ret_log_decay[l], ret_gn3, l, BS, TS, NP, True,
                          (cos_t, sin_t, state_ret_fwd, state_ret_bwd, y_ret))
        y_na, na_k, na_v = _na_ctx_call(proj, na_qn3, na_kn3, y_zero, na_k, na_v, l, BP, TP)
        bias = _na_bias(na_rpb[l], TS // GRID_W)
        y_na = _na_lat_call(proj, cache_na_k, cache_na_v, bias, na_qn3, na_kn3, y_na, l, BS, TS, NP)
        y_rw = y_zero
        for (B, T, row0, s0f, s0b) in ((BP, TP, 0, zero_st, zero_st),
                                       (BS, TS, NP, _blockdiag_states(state_rwkv_fwd[:, l]),
                                        _blockdiag_states(state_rwkv_bwd[:, l]))):
            (r, v, al, g, bonus, lw0, be0, kd0, lw1, be1, kd1) = _rw_prep_call(
                proj, hl, rw_shift, w2p, a2p, rw_g2, rw_w0, rw_a0, rw_kk3, rw_ka3, rw_rk3, l, B, T, row0)
            yf, stf = _rw_scan_call(r, v, al, lw0, be0, kd0, s0f, B, T, False)
            yb, stb = _rw_scan_call(r, v, al, lw1, be1, kd1, s0b, B, T, True)
            y_rw = _rw_post_call(yf, yb, bonus, g, rw_gn3, y_rw, l, row0)
            if row0 == 0:
                rw_f.append(_unblock_states(stf))
                rw_b.append(_unblock_states(stb))

        merged = _merge_call(h, y_ret, y_na, y_rw, w_gate, b_gate3, w_br_ret, w_br_na, w_br_rw, l)
        x = _out_call(merged, w_out, x, mod, l, 2, NP, TS)
        (h2f,) = _norm_mod_call(x, norm_ffn3, mod, l, 3, 4, NP, TS, (F32,))
        x = _moe(h2f, x, mod, router_wt, router_bias2, moe_w_gate, moe_w_up, moe_w_down, l, NP, TS)

    y_p = x[:NP].reshape(BP, TP, D)
    y_s = x[NP:].reshape(BS, TS, D)
    return (y_p, y_s, ret_f, ret_b, na_k, na_v, jnp.stack(rw_f, axis=1), jnp.stack(rw_b, axis=1))
```

```python
import functools

import jax
import jax.numpy as jnp
import numpy as np
from jax import lax
from jax.experimental import pallas as pl
from jax.experimental.pallas import tpu as pltpu

F32 = jnp.float32
BF16 = jnp.bfloat16

GRID_W = 64
RET_HEADS = 8
RET_D = 128
NA_HEADS = 8
NA_DH = 128
NA_ROWS = 8
NA_COLS = 16
RW_HEADS = 16
RW_N = 64
RW_W = RW_HEADS * RW_N
N_EXPERTS = 16
N_GROUPS = 4
EXPERTS_PER_GROUP = 4
ROPE_BASE = 10000.0
NORM_EPS = 1e-6
NEG_INF = -1e30

LANES = 128
HEAD_BLOCK = 2
RW_CHUNK = 64
RW_CB = 256
LORA_W = 640
ROW_TILE = 512
MOE_TILE = 256
DMA_UNROLL = 8
VMEM_LIMIT = 56 * 1024 * 1024


def _cparams(sem, vmem=None):
    return pltpu.CompilerParams(dimension_semantics=sem, vmem_limit_bytes=vmem)


def _dot(a, b):
    return jnp.dot(a.astype(BF16), b.astype(BF16), preferred_element_type=F32)


def _dot_nt(a, b):
    return lax.dot_general(a.astype(BF16), b.astype(BF16), (((1,), (1,)), ((), ())),
                           preferred_element_type=F32)


def _dot_tn(a, b):
    return lax.dot_general(a.astype(BF16), b.astype(BF16), (((0,), (0,)), ((), ())),
                           preferred_element_type=F32)


def _split3(x):
    h = x.astype(BF16)
    r1 = x - h.astype(F32)
    m = r1.astype(BF16)
    l = (r1 - m.astype(F32)).astype(BF16)
    return h, m, l


def _dot_exact_lhs(c, x):
    h, m, l = _split3(x)
    cb = c.astype(BF16)
    return (jnp.dot(cb, h, preferred_element_type=F32) + jnp.dot(cb, m, preferred_element_type=F32)
            + jnp.dot(cb, l, preferred_element_type=F32))


def _dot_exact_rhs(x, c):
    h, m, l = _split3(x)
    cb = c.astype(BF16)
    return (jnp.dot(h, cb, preferred_element_type=F32) + jnp.dot(m, cb, preferred_element_type=F32)
            + jnp.dot(l, cb, preferred_element_type=F32))


def _head_sum_matrix():
    r = lax.broadcasted_iota(jnp.int32, (LANES, LANES), 0) // RW_N
    c = lax.broadcasted_iota(jnp.int32, (LANES, LANES), 1) // RW_N
    return (r == c).astype(F32)


def _seg_sum(x, p):
    parts = [_dot_exact_rhs(x[:, s * LANES:(s + 1) * LANES], p) for s in range(x.shape[1] // LANES)]
    return parts[0] if len(parts) == 1 else jnp.concatenate(parts, axis=1)


def _sigmoid(x):
    return 1.0 / (1.0 + jnp.exp(-x))


def _silu(x):
    return x * _sigmoid(x)


def _mod_kernel(c_ref, w_ref, b_ref, o_ref):
    a = _silu(c_ref[...])
    o_ref[...] = _dot(a, w_ref[...]) + b_ref[...]


def _mod_call(c8, w_mod, b_mod3, l):
    D = c8.shape[1]
    n_out = w_mod.shape[2]
    tn = 512
    return pl.pallas_call(
        _mod_kernel,
        out_shape=jax.ShapeDtypeStruct((8, n_out), F32),
        grid=(n_out // tn,),
        in_specs=[pl.BlockSpec((8, D), lambda j: (0, 0)),
                  pl.BlockSpec((None, D, tn), lambda j: (l, 0, j)),
                  pl.BlockSpec((None, 1, tn), lambda j: (l, 0, j))],
        out_specs=pl.BlockSpec((8, tn), lambda j: (0, j)),
        compiler_params=_cparams(("arbitrary",), VMEM_LIMIT),
        name="mod_proj",
    )(c8, w_mod, b_mod3)


def _norm_mod_kernel(x_ref, g_ref, mod_ref, *out_refs, sh_idx, sc_idx):
    x = x_ref[...]
    y = x * lax.rsqrt(jnp.mean(x * x, -1, keepdims=True) + NORM_EPS) * g_ref[...]
    h = y * (1.0 + mod_ref[sc_idx:sc_idx + 1, :]) + mod_ref[sh_idx:sh_idx + 1, :]
    for o in out_refs:
        o[...] = h.astype(o.dtype)


def _mod_row_map(tm, n_prompt, t_s):
    def row(i):
        r = i * tm
        return jnp.where(r < n_prompt, 0, 1 + (r - n_prompt) // t_s)
    return row


def _norm_mod_call(x, g3, mod, l, sh_idx, sc_idx, n_prompt, t_s, out_dtypes):
    N, D = x.shape
    tm = 256
    row = _mod_row_map(tm, n_prompt, t_s)
    outs = tuple(jax.ShapeDtypeStruct((N, D), dt) for dt in out_dtypes)
    return pl.pallas_call(
        functools.partial(_norm_mod_kernel, sh_idx=sh_idx, sc_idx=sc_idx),
        out_shape=outs,
        grid=(N // tm,),
        in_specs=[pl.BlockSpec((tm, D), lambda i: (i, 0)),
                  pl.BlockSpec((None, 1, D), lambda i: (l, 0, 0)),
                  pl.BlockSpec((None, 6, D), lambda i: (row(i), 0, 0))],
        out_specs=tuple(pl.BlockSpec((tm, D), lambda i: (i, 0)) for _ in out_dtypes),
        compiler_params=_cparams(("arbitrary",), VMEM_LIMIT),
        name="norm_mod",
    )(x, g3, mod)


def _mm_kernel(a_ref, w_ref, o_ref, wbf_ref):
    @pl.when(pl.program_id(1) == 0)
    def _():
        wbf_ref[...] = w_ref[...].astype(BF16)

    o_ref[...] = jnp.dot(a_ref[...], wbf_ref[...], preferred_element_type=F32).astype(o_ref.dtype)


def _mm_call(a, w, l, tm, tn, name, out_dtype=F32):
    N, K = a.shape
    M = w.shape[-1]
    if w.ndim == 3:
        w_spec = pl.BlockSpec((None, K, tn), lambda j, i: (l, 0, j))
    else:
        w_spec = pl.BlockSpec((K, tn), lambda j, i: (0, j))
    return pl.pallas_call(
        _mm_kernel,
        out_shape=jax.ShapeDtypeStruct((N, M), out_dtype),
        grid=(M // tn, N // tm),
        in_specs=[pl.BlockSpec((tm, K), lambda j, i: (i, 0)), w_spec],
        out_specs=pl.BlockSpec((tm, tn), lambda j, i: (i, j)),
        scratch_shapes=[pltpu.VMEM((K, tn), BF16)],
        compiler_params=_cparams(("arbitrary", "arbitrary"), VMEM_LIMIT),
        name=name,
    )(a, w)


def _rope(x, cos, sin):
    lane = lax.broadcasted_iota(jnp.int32, x.shape, 1)
    first = (lane % 64) < 32
    rot = jnp.where(first, pltpu.roll(x, LANES - 32, 1), pltpu.roll(x, 32, 1))
    return x * cos + rot * sin


def _ret_kernel(lg_ref, q_ref, k_ref, v_ref, g_ref, gn_ref, *rest, T, tq, latent):
    if latent:
        cosq_ref, sinq_ref, cosk_ref, sink_ref, s0f_ref, s0b_ref, _, y_ref = rest
    else:
        y_ref, sf_ref, sb_ref = rest[-3:]
    hp = pl.program_id(1)
    qi = pl.program_id(2)
    hs = range(HEAD_BLOCK)
    sl = [slice(j * RET_D, (j + 1) * RET_D) for j in hs]
    lgf = [lg_ref[0, hp * HEAD_BLOCK + j] for j in hs]
    lgb = [lg_ref[1, hp * HEAD_BLOCK + j] for j in hs]
    q = [q_ref[:, s] for s in sl]
    k = [k_ref[:, s] for s in sl]
    v = [v_ref[:, s].astype(BF16) for s in sl]
    if latent:
        q = [_rope(x, cosq_ref[...], sinq_ref[...]) for x in q]
        k = [_rope(x, cosk_ref[...], sink_ref[...]) for x in k]
    k = [x * (RET_D ** -0.5) for x in k]
    t_idx = (lax.broadcasted_iota(jnp.int32, (tq, T), 0) + qi * tq).astype(F32)
    s_idx = lax.broadcasted_iota(jnp.int32, (tq, T), 1).astype(F32)
    diff = t_idx - s_idx
    fwd, bwd = jnp.maximum(diff, 0.0), jnp.maximum(-diff, 0.0)
    dmask = [jnp.where(diff >= 0, jnp.exp(lgf[j] * fwd), 0.0) + jnp.where(diff <= 0, jnp.exp(lgb[j] * bwd), 0.0)
             for j in hs]
    scores = [_dot_nt(q[j], k[j]) * dmask[j] for j in hs]
    y = [_dot(scores[j], v[j]) for j in hs]
    if latent:
        tq_col = (lax.broadcasted_iota(jnp.int32, (tq, 1), 0) + qi * tq).astype(F32)
        y = [y[j] + _dot(q[j] * jnp.exp(lgf[j] * (tq_col + 1.0)), s0f_ref[j])
             + _dot(q[j] * jnp.exp(lgb[j] * (T - tq_col)), s0b_ref[j]) for j in hs]
    else:
        s_col = lax.broadcasted_iota(jnp.int32, (T, 1), 0).astype(F32)
        for j in hs:
            sf_ref[j] = _dot_tn(k[j] * jnp.exp(lgf[j] * (T - 1.0 - s_col)), v[j])
            sb_ref[j] = _dot_tn(k[j] * jnp.exp(lgb[j] * s_col), v[j])
    for j in hs:
        mu = jnp.mean(y[j], -1, keepdims=True)
        yc = y[j] - mu
        var = jnp.mean(yc * yc, -1, keepdims=True)
        yn = yc * lax.rsqrt(var + 1e-5) * gn_ref[:, sl[j]]
        y_ref[:, sl[j]] = (yn * _silu(g_ref[:, sl[j]])).astype(y_ref.dtype)


def _ret_call(proj, lg, gn3, l, B, T, row0, latent, extra):
    N = proj.shape[0]
    tq = 256
    nq = T // tq
    rb = row0 // T
    HB = HEAD_BLOCK
    nh = RET_HEADS // HB
    bw = HB * RET_D
    qmap = lambda b, h, qi: (rb * nq + b * nq + qi, h)
    kmap = lambda b, h, qi: (rb + b, nh + h)
    vmap = lambda b, h, qi: (rb + b, 2 * nh + h)
    gmap = lambda b, h, qi: (rb * nq + b * nq + qi, 3 * nh + h)
    in_specs = [pl.BlockSpec(memory_space=pltpu.SMEM),
                pl.BlockSpec((tq, bw), qmap),
                pl.BlockSpec((T, bw), kmap),
                pl.BlockSpec((T, bw), vmap),
                pl.BlockSpec((tq, bw), gmap),
                pl.BlockSpec((None, 1, bw), lambda b, h, qi: (l, 0, h))]
    args = [lg, proj, proj, proj, proj, gn3]
    y_spec = pl.BlockSpec((tq, bw), qmap)
    y_shape = jax.ShapeDtypeStruct((N, RET_HEADS * RET_D), BF16)
    st_blk = pl.BlockSpec((None, None, HB, RET_D, RET_D), lambda b, h, qi: (b, l, h, 0, 0))
    if latent:
        cos, sin, s0f, s0b, y_buf = extra
        in_specs += [pl.BlockSpec((tq, RET_D), lambda b, h, qi: (qi, 0)),
                     pl.BlockSpec((tq, RET_D), lambda b, h, qi: (qi, 0)),
                     pl.BlockSpec((T, RET_D), lambda b, h, qi: (0, 0)),
                     pl.BlockSpec((T, RET_D), lambda b, h, qi: (0, 0)),
                     st_blk, st_blk, pl.BlockSpec(memory_space=pl.ANY)]
        args += [cos, sin, cos, sin, s0f, s0b, y_buf]
        aliases = {len(args) - 1: 0}
        out_shape = y_shape
        out_specs = y_spec
    else:
        y_buf, sf_buf, sb_buf = extra
        st = jax.ShapeDtypeStruct(sf_buf.shape, F32)
        st_spec = st_blk
        in_specs += [pl.BlockSpec(memory_space=pl.ANY)] * 3
        args += [y_buf, sf_buf, sb_buf]
        aliases = {len(args) - 3: 0, len(args) - 2: 1, len(args) - 1: 2}
        out_shape = (y_shape, st, st)
        out_specs = (y_spec, st_spec, st_spec)
    return pl.pallas_call(
        functools.partial(_ret_kernel, T=T, tq=tq, latent=latent),
        out_shape=out_shape,
        grid=(B, nh, nq),
        in_specs=in_specs,
        out_specs=out_specs,
        input_output_aliases=aliases,
        compiler_params=_cparams(("arbitrary", "arbitrary", "arbitrary"), VMEM_LIMIT),
        name="retention_latent" if latent else "retention_ctx",
    )(*args)


def _rms(x, g):
    return x * lax.rsqrt(jnp.mean(x * x, -1, keepdims=True) + NORM_EPS) * g


def _na_ctx_kernel(q_ref, k_ref, v_ref, qn_ref, kn_ref, *rest):
    y_ref, ko_ref, vo_ref = rest[-3:]
    hs = range(HEAD_BLOCK)
    sl = [slice(j * NA_DH, (j + 1) * NA_DH) for j in hs]
    q = [_rms(q_ref[:, s], qn_ref[...]) * (NA_DH ** -0.5) for s in sl]
    k = [_rms(k_ref[:, s], kn_ref[...]) for s in sl]
    v = [v_ref[:, s] for s in sl]
    s = [_dot_nt(q[j], k[j]) for j in hs]
    p = [jnp.exp(x - jnp.max(x, -1, keepdims=True)) for x in s]
    o = [_dot(p[j], v[j]) for j in hs]
    for j in hs:
        y_ref[:, sl[j]] = (o[j] / jnp.sum(p[j], -1, keepdims=True)).astype(y_ref.dtype)
        ko_ref[j] = k[j]
        vo_ref[j] = v[j]


def _na_ctx_call(proj, qn3, kn3, y_buf, k_buf, v_buf, l, B, T):
    N = proj.shape[0]
    HB = HEAD_BLOCK
    nh = NA_HEADS // HB
    bw = HB * NA_DH
    c0 = 4 * (RET_HEADS // HB)
    kv = jax.ShapeDtypeStruct(k_buf.shape, F32)
    kv_spec = pl.BlockSpec((None, None, HB, T, NA_DH), lambda b, h: (b, l, h, 0, 0))
    any_spec = pl.BlockSpec(memory_space=pl.ANY)
    return pl.pallas_call(
        _na_ctx_kernel,
        out_shape=(jax.ShapeDtypeStruct((N, NA_HEADS * NA_DH), BF16), kv, kv),
        grid=(B, nh),
        in_specs=[pl.BlockSpec((T, bw), lambda b, h: (b, c0 + h)),
                  pl.BlockSpec((T, bw), lambda b, h: (b, c0 + nh + h)),
                  pl.BlockSpec((T, bw), lambda b, h: (b, c0 + 2 * nh + h)),
                  pl.BlockSpec((None, 1, NA_DH), lambda b, h: (l, 0, 0)),
                  pl.BlockSpec((None, 1, NA_DH), lambda b, h: (l, 0, 0)),
                  any_spec, any_spec, any_spec],
        out_specs=(pl.BlockSpec((T, bw), lambda b, h: (b, h)), kv_spec, kv_spec),
        input_output_aliases={5: 0, 6: 1, 7: 2},
        compiler_params=_cparams(("arbitrary", "arbitrary"), VMEM_LIMIT),
        name="na_ctx",
    )(proj, proj, proj, qn3, kn3, y_buf, k_buf, v_buf)


def _na_lat_kernel(q_ref, k_ref, v_ref, ck_ref, cv_ref, bias_ref, qn_ref, kn_ref, _, y_ref, *, rows, kr):
    q = _rms(q_ref[...], qn_ref[...]) * (NA_DH ** -0.5)
    k = _rms(k_ref[...], kn_ref[...]).astype(BF16)
    v = v_ref[...].astype(BF16)
    ck = ck_ref[...].astype(BF16)
    cv = cv_ref[...].astype(BF16)
    nl = kr * GRID_W
    qb = q.astype(BF16)
    starts = [min(max(r - kr // 2, 0), rows - kr) for r in range(rows)]
    groups = []
    for r, rs in enumerate(starts):
        if groups and groups[-1][0] == rs:
            groups[-1][2] = r + 1
        else:
            groups.append([rs, r, r + 1])
    W = GRID_W
    s_ctx = _dot_nt(qb, ck)
    s_loc = [_dot_nt(qb[r0 * W:r1 * W], k[rs * W:rs * W + nl])
             + bias_ref[r0:r1].reshape((r1 - r0) * W, nl) for rs, r0, r1 in groups]
    p_loc, p_ctx, den = [], [], []
    for (rs, r0, r1), sl in zip(groups, s_loc):
        sc = s_ctx[r0 * W:r1 * W]
        m = jnp.maximum(jnp.max(sl, -1, keepdims=True), jnp.max(sc, -1, keepdims=True))
        pl_, pc_ = jnp.exp(sl - m), jnp.exp(sc - m)
        p_loc.append(pl_.astype(BF16))
        p_ctx.append(pc_.astype(BF16))
        den.append(jnp.sum(pl_, -1, keepdims=True) + jnp.sum(pc_, -1, keepdims=True))
    o_ctx = _dot(jnp.concatenate(p_ctx, axis=0), cv)
    o_loc = [_dot(p, v[rs * W:rs * W + nl]) for (rs, r0, r1), p in zip(groups, p_loc)]
    for (rs, r0, r1), ol, d in zip(groups, o_loc, den):
        y_ref[r0 * W:r1 * W, :] = ((ol + o_ctx[r0 * W:r1 * W]) / d).astype(y_ref.dtype)


def _na_bias(rpb, rows):
    H = rpb.shape[0]
    kr = min(NA_ROWS, rows)
    r_ids = np.arange(rows)
    row_start = np.clip(r_ids - kr // 2, 0, rows - kr)
    row_idx = row_start[:, None] + np.arange(kr)[None, :]
    cols = np.arange(GRID_W)
    col_start = np.clip(cols - NA_COLS // 2, 0, GRID_W - NA_COLS)
    col_ok = (cols[None, :] >= col_start[:, None]) & (cols[None, :] < col_start[:, None] + NA_COLS)
    dr_idx = row_idx - r_ids[:, None] + (NA_ROWS - 1)
    dc_idx = np.clip(cols[None, :] - cols[:, None] + (NA_COLS - 1), 0, 2 * NA_COLS - 2)
    rpb = rpb.astype(F32)
    blk = jnp.zeros((H, 2 * NA_ROWS - 1, GRID_W, GRID_W), F32)
    for d in range(2 * NA_COLS - 1):
        blk = jnp.where(jnp.asarray(dc_idx == d)[None, None], rpb[:, :, d][:, :, None, None], blk)
    blk = jnp.where(jnp.asarray(col_ok)[None, None], blk, NEG_INF)
    rows_out = [jnp.concatenate([blk[:, int(dr_idx[r, j])] for j in range(kr)], axis=-1) for r in range(rows)]
    return jnp.stack(rows_out, axis=1)


def _na_lat_call(proj, cache_k, cache_v, bias, qn3, kn3, y_prev, l, B, T, row0):
    N = proj.shape[0]
    c0 = 4 * RET_HEADS
    rb = row0 // T
    rows = T // GRID_W
    kr = min(NA_ROWS, rows)
    L = cache_k.shape[3]
    return pl.pallas_call(
        functools.partial(_na_lat_kernel, rows=rows, kr=kr),
        out_shape=jax.ShapeDtypeStruct((N, NA_HEADS * NA_DH), BF16),
        grid=(B, NA_HEADS),
        in_specs=[pl.BlockSpec((T, NA_DH), lambda b, h: (rb + b, c0 + h)),
                  pl.BlockSpec((T, NA_DH), lambda b, h: (rb + b, c0 + NA_HEADS + h)),
                  pl.BlockSpec((T, NA_DH), lambda b, h: (rb + b, c0 + 2 * NA_HEADS + h)),
                  pl.BlockSpec((None, None, None, L, NA_DH), lambda b, h: (b, l, h, 0, 0)),
                  pl.BlockSpec((None, None, None, L, NA_DH), lambda b, h: (b, l, h, 0, 0)),
                  pl.BlockSpec((None, rows, GRID_W, kr * GRID_W), lambda b, h: (h, 0, 0, 0)),
                  pl.BlockSpec((None, 1, NA_DH), lambda b, h: (l, 0, 0)),
                  pl.BlockSpec((None, 1, NA_DH), lambda b, h: (l, 0, 0)),
                  pl.BlockSpec(memory_space=pl.ANY)],
        out_specs=pl.BlockSpec((T, NA_DH), lambda b, h: (rb + b, h)),
        input_output_aliases={8: 0},
        compiler_params=_cparams(("arbitrary", "arbitrary"), VMEM_LIMIT),
        name="na_latent",
    )(proj, proj, proj, cache_k, cache_v, bias, qn3, kn3, y_prev)


def _rw_prep_kernel(xr_ref, xk_ref, xv_ref, shr_ref, shk_ref, shv_ref, hl_ref, w2_ref, a2_ref, g2_ref,
                    w0_ref, a0_ref, kk_ref, ka_ref, rk_ref,
                    r_o, v_o, al_o, g_o, bon_o, lw0_o, be0_o, kd0_o, lw1_o, be1_o, kd1_o, *, T):
    row = lax.broadcasted_iota(jnp.int32, (T, 1), 0)

    def shift(x_ref, sh_ref):
        x = x_ref[...]
        prev = jnp.where(row == 0, 0.0, pltpu.roll(x, 1, 0))
        nxt = jnp.where(row == T - 1, 0.0, pltpu.roll(x, T - 1, 0))
        return sh_ref[0:1, :] * prev + sh_ref[1:2, :] * x + sh_ref[2:3, :] * nxt

    r = shift(xr_ref, shr_ref)
    k = shift(xk_ref, shk_ref)
    v = shift(xv_ref, shv_ref)
    p = _head_sum_matrix()
    hl = hl_ref[...]
    g = _dot(_sigmoid(hl[:, 4 * LANES:5 * LANES]), g2_ref[...])
    kk = k * kk_ref[...]
    kk = kk / jnp.maximum(jnp.sqrt(_seg_sum(kk * kk, p)), 1e-6)
    bonus = _seg_sum(r * k * rk_ref[...], p) * v
    r_o[...] = r
    v_o[...] = v
    al_o[...] = -kk
    g_o[...] = g
    bon_o[...] = bonus
    outs = ((lw0_o, be0_o, kd0_o), (lw1_o, be1_o, kd1_o))
    for d in range(2):
        z = w0_ref[d:d + 1, :] + _dot(jnp.tanh(hl[:, d * LANES:(d + 1) * LANES]), w2_ref[d])
        u = -z
        softplus = jnp.maximum(u, 0.0) + jnp.log(1.0 + jnp.exp(-jnp.abs(u)))
        wlog = -softplus - 0.5
        a = _sigmoid(a0_ref[d:d + 1, :] + _dot(hl[:, (2 + d) * LANES:(3 + d) * LANES], a2_ref[d]))
        lw_o, be_o, kd_o = outs[d]
        lw_o[...] = -jnp.exp(wlog)
        be_o[...] = kk * a
        kd_o[...] = k * (1.0 + (a - 1.0) * ka_ref[...])


def _rw_prep_call(proj, hl, shift3, w2p, a2p, g2, w0, a0, kk3, ka3, rk3, l, B, T, row0):
    rb = row0 // T
    nc = RW_W // RW_CB
    c0 = 7 * (RW_W // RW_CB)
    n_rows = B * T
    big = lambda off: pl.BlockSpec((T, RW_CB), lambda b, c: (rb + b, c0 + off * nc + c))
    sh = lambda off: pl.BlockSpec((None, 3, RW_CB), lambda b, c: (l, 0, off * nc + c))
    vec = pl.BlockSpec((None, 1, RW_CB), lambda b, c: (l, 0, c))
    out_spec = pl.BlockSpec((T, RW_CB), lambda b, c: (b, c))
    out = jax.ShapeDtypeStruct((n_rows, RW_W), F32)
    return pl.pallas_call(
        functools.partial(_rw_prep_kernel, T=T),
        out_shape=(out,) * 11,
        grid=(B, nc),
        in_specs=[big(0), big(1), big(2), sh(0), sh(1), sh(2),
                  pl.BlockSpec((T, LORA_W), lambda b, c: (rb + b, 0)),
                  pl.BlockSpec((None, 2, LANES, RW_CB), lambda b, c: (l, 0, 0, c)),
                  pl.BlockSpec((None, 2, LANES, RW_CB), lambda b, c: (l, 0, 0, c)),
                  pl.BlockSpec((None, LANES, RW_CB), lambda b, c: (l, 0, c)),
                  pl.BlockSpec((None, 2, RW_CB), lambda b, c: (l, 0, c)),
                  pl.BlockSpec((None, 2, RW_CB), lambda b, c: (l, 0, c)),
                  vec, vec, vec],
        out_specs=(out_spec,) * 11,
        compiler_params=_cparams(("arbitrary", "arbitrary"), VMEM_LIMIT),
        name="rwkv_prep",
    )(proj, proj, proj, shift3, shift3, shift3, hl, w2p, a2p, g2, w0, a0, kk3, ka3, rk3)


def _rw_scan_kernel(r_ref, v_ref, al_ref, lw_ref, be_ref, kd_ref, s0_ref, y_ref, sf_ref, st_ref,
                    *, reverse, nchunks):
    C = RW_CHUNK
    c = pl.program_id(1)

    @pl.when(c == 0)
    def _():
        st_ref[...] = s0_ref[...]

    ti = lax.broadcasted_iota(jnp.int32, (C, C), 0)
    si = lax.broadcasted_iota(jnp.int32, (C, C), 1)
    incl = (ti <= si) if reverse else (ti >= si)
    lw = lw_ref[...]
    cum = _dot_exact_lhs(incl.astype(F32), lw)
    tot = jnp.sum(lw, axis=0, keepdims=True)
    e_incl = jnp.exp(cum)
    e_excl = jnp.exp(cum - lw)
    e_neg = jnp.exp(-cum)
    e_rem = jnp.exp(tot - cum)
    g_tot = jnp.exp(tot)
    be = be_ref[...]
    kd = kd_ref[...]
    scaled = (al_ref[...] * e_excl, r_ref[...] * e_incl, be * e_neg, kd * e_neg, v_ref[...],
              be * e_rem, kd * e_rem)

    S = 2 * C
    t2 = lax.broadcasted_iota(jnp.int32, (S, S), 0)
    s2 = lax.broadcasted_iota(jnp.int32, (S, S), 1)
    same = (t2 // C) == (s2 // C)
    if reverse:
        strict2 = same & (t2 < s2)
        incl2 = same & (t2 <= s2)
    else:
        strict2 = same & (t2 > s2)
        incl2 = same & (t2 >= s2)
    eye = t2 == s2
    lane = lax.broadcasted_iota(jnp.int32, (C, LANES), 1)
    m0 = (lane < RW_N).astype(F32)
    m1 = 1.0 - m0

    def stack(x):
        return jnp.concatenate([x * m0, x * m1], axis=0).astype(BF16)

    pairs = range(RW_HEADS // 2)
    cat = jnp.concatenate
    sls = [slice(p * LANES, (p + 1) * LANES) for p in pairs]
    a_s, r_s, b_s, k_s, v_s, bg_s, kg_s = ([stack(x[:, sl]) for sl in sls] for x in scaled)
    hst = [st_ref[p] for p in pairs]
    h_b = [h.astype(BF16) for h in hst]
    m_all = [_dot_nt(cat([a_s[p], r_s[p]], 0), cat([b_s[p], k_s[p]], 0)) for p in pairs]
    m_ab = [jnp.where(strict2, m_all[p][:S, :S], 0.0) for p in pairs]
    m_ak = [jnp.where(strict2, m_all[p][:S, S:], 0.0).astype(BF16) for p in pairs]
    m_rb = [jnp.where(incl2, m_all[p][S:, :S], 0.0).astype(BF16) for p in pairs]
    m_rk = [jnp.where(incl2, m_all[p][S:, S:], 0.0).astype(BF16) for p in pairs]
    x = [_dot(cat([a_s[p], m_ak[p]], 1), cat([h_b[p], v_s[p]], 0)) for p in pairs]
    pw = m_ab
    n_steps = int(np.log2(C))
    for step in range(n_steps):
        if step + 1 < n_steps:
            px = [_dot(pw[p], cat([x[p], pw[p]], 1)) for p in pairs]
            x = [x[p] + px[p][:, :S] for p in pairs]
            pw = [px[p][:, S:] for p in pairs]
        else:
            x = [x[p] + _dot(pw[p], x[p]) for p in pairs]
    u_s = [x[p].astype(BF16) for p in pairs]
    y_s = [_dot(cat([r_s[p], m_rb[p], m_rk[p]], 1), cat([h_b[p], u_s[p], v_s[p]], 0)) for p in pairs]
    upd = [_dot_tn(cat([bg_s[p], kg_s[p]], 0), cat([u_s[p], v_s[p]], 0)) for p in pairs]
    for p in pairs:
        y_ref[:, sls[p]] = y_s[p][:C] + y_s[p][C:]
        g_col = jnp.sum(jnp.where(eye, jnp.broadcast_to(g_tot[:, sls[p]], (S, S)), 0.0), axis=1, keepdims=True)
        st_ref[p] = hst[p] * g_col + upd[p]

    @pl.when(c == nchunks - 1)
    def _():
        sf_ref[...] = st_ref[...]


def _rw_scan_call(r, v, al, lw, be, kd, s0, B, T, reverse):
    C = RW_CHUNK
    nch = T // C
    if reverse:
        cmap = lambda b, c: (b * nch + (nch - 1 - c), 0)
    else:
        cmap = lambda b, c: (b * nch + c, 0)
    blk = pl.BlockSpec((C, RW_W), cmap)
    st_spec = pl.BlockSpec((None, RW_HEADS // 2, LANES, LANES), lambda b, c: (b, 0, 0, 0))
    return pl.pallas_call(
        functools.partial(_rw_scan_kernel, reverse=reverse, nchunks=nch),
        out_shape=(jax.ShapeDtypeStruct((B * T, RW_W), F32),
                   jax.ShapeDtypeStruct((B, RW_HEADS // 2, LANES, LANES), F32)),
        grid=(B, nch),
        in_specs=[blk] * 6 + [st_spec],
        out_specs=(blk, st_spec),
        scratch_shapes=[pltpu.VMEM((RW_HEADS // 2, LANES, LANES), F32)],
        compiler_params=_cparams(("arbitrary", "arbitrary"), VMEM_LIMIT),
        name="rwkv_scan_bwd" if reverse else "rwkv_scan_fwd",
    )(r, v, al, lw, be, kd, s0)


def _rw_post_kernel(yf_ref, yb_ref, bon_ref, g_ref, gn_ref, *rest):
    o_ref = rest[-1]
    p = _head_sum_matrix()
    y = yf_ref[...] + yb_ref[...]
    mu = _seg_sum(y, p) * (1.0 / RW_N)
    yc = y - mu
    var = _seg_sum(yc * yc, p) * (1.0 / RW_N)
    yn = yc * lax.rsqrt(var + 64e-5) * gn_ref[...]
    o_ref[...] = ((yn + bon_ref[...]) * g_ref[...]).astype(o_ref.dtype)


def _rw_post_call(yf, yb, bonus, g, gn3, y_buf, l, row0):
    n_rows = yf.shape[0]
    tm = 256
    blk = pl.BlockSpec((tm, RW_W), lambda i: (i, 0))
    return pl.pallas_call(
        _rw_post_kernel,
        out_shape=jax.ShapeDtypeStruct(y_buf.shape, BF16),
        grid=(n_rows // tm,),
        in_specs=[blk, blk, blk, blk, pl.BlockSpec((None, 1, RW_W), lambda i: (l, 0, 0)),
                  pl.BlockSpec(memory_space=pl.ANY)],
        out_specs=pl.BlockSpec((tm, RW_W), lambda i: (row0 // tm + i, 0)),
        input_output_aliases={5: 0},
        compiler_params=_cparams(("arbitrary",), VMEM_LIMIT),
        name="rwkv_post",
    )(yf, yb, bonus, g, gn3, y_buf)


def _blockdiag_states(s):
    B = s.shape[0]
    s = jnp.swapaxes(s, -1, -2).reshape(B, RW_HEADS // 2, 2, RW_N, RW_N)
    z = jnp.zeros_like(s[:, :, 0])
    top = jnp.concatenate([s[:, :, 0], z], axis=-1)
    bot = jnp.concatenate([z, s[:, :, 1]], axis=-1)
    return jnp.concatenate([top, bot], axis=-2)


def _unblock_states(sd):
    B = sd.shape[0]
    h0 = sd[:, :, :RW_N, :RW_N]
    h1 = sd[:, :, RW_N:, RW_N:]
    return jnp.swapaxes(jnp.stack([h0, h1], axis=2).reshape(B, RW_HEADS, RW_N, RW_N), -1, -2)


def _merge_kernel(h_ref, ya_ref, yb_ref, yc_ref, wga_ref, wgb_ref, wgc_ref, bga_ref, bgb_ref, bgc_ref,
                  wa_ref, wb_ref, wc_ref, o_ref, wg_s, wbr_s):
    @pl.when(pl.program_id(1) == 0)
    def _():
        for n, (wg, wb) in enumerate(((wga_ref, wa_ref), (wgb_ref, wb_ref), (wgc_ref, wc_ref))):
            wg_s[n] = wg[...].astype(BF16)
            wbr_s[n] = wb[...].astype(BF16)

    h = h_ref[...]
    acc = None
    for n, (y_ref, bg_ref) in enumerate(((ya_ref, bga_ref), (yb_ref, bgb_ref), (yc_ref, bgc_ref))):
        gate = _sigmoid(jnp.dot(h, wg_s[n], preferred_element_type=F32) + bg_ref[...])
        term = gate * jnp.dot(y_ref[...], wbr_s[n], preferred_element_type=F32)
        acc = term if acc is None else acc + term
    o_ref[...] = acc.astype(o_ref.dtype)


def _merge_call(h, y_ret, y_na, y_rw, w_gate, b_gate3, w_ret, w_na, w_rw, l):
    N, D = h.shape
    KB = y_ret.shape[1]
    tm, tn = ROW_TILE, 256
    nj = D // tn
    a_spec = pl.BlockSpec((tm, D), lambda j, i: (i, 0))
    y_spec = pl.BlockSpec((tm, KB), lambda j, i: (i, 0))
    once = pl.Buffered(1)
    wg = lambda n: pl.BlockSpec((None, D, tn), lambda j, i: (l, 0, n * nj + j))
    bg = lambda n: pl.BlockSpec((None, 1, tn), lambda j, i: (l, 0, n * nj + j))
    wb = pl.BlockSpec((None, KB, tn), lambda j, i: (l, 0, j), pipeline_mode=once)
    return pl.pallas_call(
        _merge_kernel,
        out_shape=jax.ShapeDtypeStruct((N, D), BF16),
        grid=(nj, N // tm),
        in_specs=[a_spec, y_spec, y_spec, y_spec, wg(0), wg(1), wg(2), bg(0), bg(1), bg(2), wb, wb, wb],
        out_specs=pl.BlockSpec((tm, tn), lambda j, i: (i, j)),
        scratch_shapes=[pltpu.VMEM((3, D, tn), BF16), pltpu.VMEM((3, KB, tn), BF16)],
        compiler_params=_cparams(("arbitrary", "arbitrary"), VMEM_LIMIT),
        name="gated_merge",
    )(h, y_ret, y_na, y_rw, w_gate, w_gate, w_gate, b_gate3, b_gate3, b_gate3, w_ret, w_na, w_rw)


def _out_kernel(a_ref, w_ref, x_ref, mod_ref, o_ref, wbf_ref, *, g_idx):
    @pl.when(pl.program_id(1) == 0)
    def _():
        wbf_ref[...] = w_ref[...].astype(BF16)

    m = jnp.dot(a_ref[...], wbf_ref[...], preferred_element_type=F32)
    o_ref[...] = x_ref[...] + mod_ref[g_idx:g_idx + 1, :] * m


def _out_call(a, w_out, x, mod, l, g_idx, n_prompt, t_s):
    N, D = x.shape
    tm, tn = 2 * ROW_TILE, 512
    row = _mod_row_map(tm, n_prompt, t_s)
    return pl.pallas_call(
        functools.partial(_out_kernel, g_idx=g_idx),
        out_shape=jax.ShapeDtypeStruct((N, D), F32),
        grid=(D // tn, N // tm),
        in_specs=[pl.BlockSpec((tm, D), lambda j, i: (i, 0)),
                  pl.BlockSpec((None, D, tn), lambda j, i: (l, 0, j)),
                  pl.BlockSpec((tm, tn), lambda j, i: (i, j)),
                  pl.BlockSpec((None, 6, tn), lambda j, i: (row(i), 0, j))],
        out_specs=pl.BlockSpec((tm, tn), lambda j, i: (i, j)),
        scratch_shapes=[pltpu.VMEM((D, tn), BF16)],
        compiler_params=_cparams(("arbitrary", "arbitrary"), VMEM_LIMIT),
        name="out_proj",
    )(a, w_out, x, mod)


def _router_kernel(h_ref, wt_ref, bias_ref, idx_ref, wgt_ref):
    h1, h2, _ = _split3(h_ref[...])
    w1, w2, _ = _split3(wt_ref[...])
    nt = lambda a, b: lax.dot_general(a, b, (((1,), (1,)), ((), ())), preferred_element_type=F32)
    logits = nt(w1, h1) + nt(w1, h2) + nt(w2, h1)
    scores = _sigmoid(logits)
    sel = scores + bias_ref[...]
    row = lambda a, e: a[e:e + 1, :]
    grp = []
    for g in range(N_GROUPS):
        v = [row(sel, g * EXPERTS_PER_GROUP + j) for j in range(EXPERTS_PER_GROUP)]
        best = None
        for i in range(EXPERTS_PER_GROUP):
            for j in range(i + 1, EXPERTS_PER_GROUP):
                s = v[i] + v[j]
                best = s if best is None else jnp.maximum(best, s)
        grp.append(best)
    gbest = jnp.zeros_like(grp[0], dtype=jnp.int32)
    gval = grp[0]
    for g in range(1, N_GROUPS):
        better = grp[g] > gval
        gbest = jnp.where(better, g, gbest)
        gval = jnp.where(better, grp[g], gval)
    neg = jnp.full_like(gval, -jnp.inf)
    masked = [jnp.where(gbest == (e // EXPERTS_PER_GROUP), row(sel, e), neg) for e in range(N_EXPERTS)]

    def top(excl):
        bi = jnp.full_like(gbest, -1)
        bv = neg
        bs = jnp.zeros_like(gval)
        for e in range(N_EXPERTS):
            cand = masked[e] if excl is None else jnp.where(excl == e, neg, masked[e])
            better = cand > bv
            bi = jnp.where(better, e, bi)
            bv = jnp.where(better, cand, bv)
            bs = jnp.where(better, row(scores, e), bs)
        return bi, bs

    i1, s1 = top(None)
    i2, s2 = top(i1)
    den = s1 + s2
    idx_ref[...] = jnp.concatenate([i1, i2] + [jnp.zeros_like(i1)] * 6, axis=0)
    wgt_ref[...] = jnp.concatenate([s1 / den, s2 / den] + [jnp.zeros_like(s1)] * 6, axis=0)


def _router_call(h2, router_wt, router_bias2):
    N, D = h2.shape
    tm = 512
    return pl.pallas_call(
        _router_kernel,
        out_shape=(jax.ShapeDtypeStruct((8, N), jnp.int32), jax.ShapeDtypeStruct((8, N), F32)),
        grid=(N // tm,),
        in_specs=[pl.BlockSpec((tm, D), lambda i: (i, 0)),
                  pl.BlockSpec((N_EXPERTS, D), lambda i: (0, 0)),
                  pl.BlockSpec((N_EXPERTS, 1), lambda i: (0, 0))],
        out_specs=(pl.BlockSpec((8, tm), lambda i: (0, i)), pl.BlockSpec((8, tm), lambda i: (0, i))),
        compiler_params=_cparams(("arbitrary",), VMEM_LIMIT),
        name="moe_router",
    )(h2, router_wt, router_bias2)


def _row_copy(src_hbm, dst, sem, src_row, dst_row):
    return pltpu.make_async_copy(src_hbm.at[pl.ds(src_row, 1)], dst.at[pl.ds(dst_row, 1)], sem)


def _last_used(i, meta, n_tiles):
    return jnp.minimum(i, meta[n_tiles] - 1)


def _gather_kernel(tok_ref, meta_ref, x_hbm, o_ref, buf, sem, *, tm, n_tiles):
    i = pl.program_id(0)
    n_used = meta_ref[n_tiles]

    def issue(tile, slot):
        base = tile * tm

        def body(g, carry):
            for u in range(DMA_UNROLL):
                r = g * DMA_UNROLL + u
                _row_copy(x_hbm, buf.at[slot], sem.at[slot], tok_ref[base + r], r).start(priority=u % 2)
            return carry

        lax.fori_loop(0, tm // DMA_UNROLL, body, 0)

    @pl.when(i == 0)
    def _():
        issue(0, 0)

    @pl.when(i + 1 < n_used)
    def _():
        issue(i + 1, (i + 1) % 2)

    @pl.when(i < n_used)
    def _():
        slot = i % 2
        pltpu.make_async_copy(x_hbm.at[pl.ds(0, tm)], buf.at[slot], sem.at[slot]).wait()
        o_ref[...] = buf[slot].astype(o_ref.dtype)

    @pl.when(i >= n_used)
    def _():
        o_ref[...] = jnp.zeros_like(o_ref)


def _gather_call(row_tok, meta, x, n_tiles):
    D = x.shape[1]
    tm = MOE_TILE
    return pl.pallas_call(
        functools.partial(_gather_kernel, tm=tm, n_tiles=n_tiles),
        out_shape=jax.ShapeDtypeStruct((n_tiles * tm, D), BF16),
        grid_spec=pltpu.PrefetchScalarGridSpec(
            num_scalar_prefetch=2,
            grid=(n_tiles,),
            in_specs=[pl.BlockSpec(memory_space=pl.ANY)],
            out_specs=pl.BlockSpec((tm, D), lambda i, tok, meta: (i, 0)),
            scratch_shapes=[pltpu.VMEM((2, tm, D), F32), pltpu.SemaphoreType.DMA((2,))]),
        compiler_params=_cparams(("arbitrary",), VMEM_LIMIT),
        name="moe_gather",
    )(row_tok, meta, x)


def _expert_changed(meta_ref, i):
    return jnp.logical_or(i == 0, meta_ref[i] != meta_ref[jnp.maximum(i - 1, 0)])


def _expert_up_kernel(meta_ref, x_ref, wg_ref, wu_ref, o_ref, wg_s, wu_s, *, n_tiles):
    i = pl.program_id(1)

    @pl.when(_expert_changed(meta_ref, i))
    def _():
        wg_s[...] = wg_ref[...].astype(BF16)
        wu_s[...] = wu_ref[...].astype(BF16)

    @pl.when(i < meta_ref[n_tiles])
    def _():
        x = x_ref[...]
        a = jnp.dot(x, wg_s[...], preferred_element_type=F32)
        b = jnp.dot(x, wu_s[...], preferred_element_type=F32)
        o_ref[...] = (_silu(a) * b).astype(o_ref.dtype)

    @pl.when(i >= meta_ref[n_tiles])
    def _():
        o_ref[...] = jnp.zeros_like(o_ref)


def _expert_up_call(meta, xs, w_gate, w_up, l, n_tiles):
    D = xs.shape[1]
    FF = w_gate.shape[-1]
    tm, tf = MOE_TILE, 512
    w_spec = pl.BlockSpec((None, None, D, tf), lambda f, i, meta: (l, meta[i], 0, f))
    return pl.pallas_call(
        functools.partial(_expert_up_kernel, n_tiles=n_tiles),
        out_shape=jax.ShapeDtypeStruct((n_tiles * tm, FF), BF16),
        grid_spec=pltpu.PrefetchScalarGridSpec(
            num_scalar_prefetch=1,
            grid=(FF // tf, n_tiles),
            in_specs=[pl.BlockSpec((tm, D), lambda f, i, meta: (_last_used(i, meta, n_tiles), 0)),
                      w_spec, w_spec],
            out_specs=pl.BlockSpec((tm, tf), lambda f, i, meta: (i, f)),
            scratch_shapes=[pltpu.VMEM((D, tf), BF16), pltpu.VMEM((D, tf), BF16)]),
        compiler_params=_cparams(("arbitrary", "arbitrary"), VMEM_LIMIT),
        name="moe_expert_up",
    )(meta, xs, w_gate, w_up)


def _expert_down_kernel(meta_ref, h_ref, wd_ref, o_ref, wd_s, *, n_tiles):
    i = pl.program_id(1)

    @pl.when(_expert_changed(meta_ref, i))
    def _():
        wd_s[...] = wd_ref[...].astype(BF16)

    @pl.when(i < meta_ref[n_tiles])
    def _():
        o_ref[...] = jnp.dot(h_ref[...], wd_s[...], preferred_element_type=F32)

    @pl.when(i >= meta_ref[n_tiles])
    def _():
        o_ref[...] = jnp.zeros_like(o_ref)


def _expert_down_call(meta, hmid, w_down, l, n_tiles):
    FF = hmid.shape[1]
    D = w_down.shape[-1]
    tm, tn = MOE_TILE, D
    return pl.pallas_call(
        functools.partial(_expert_down_kernel, n_tiles=n_tiles),
        out_shape=jax.ShapeDtypeStruct((n_tiles * tm, D), F32),
        grid_spec=pltpu.PrefetchScalarGridSpec(
            num_scalar_prefetch=1,
            grid=(D // tn, n_tiles),
            in_specs=[pl.BlockSpec((tm, FF), lambda n, i, meta: (_last_used(i, meta, n_tiles), 0)),
                      pl.BlockSpec((None, None, FF, tn), lambda n, i, meta: (l, meta[i], 0, n))],
            out_specs=pl.BlockSpec((tm, tn), lambda n, i, meta: (i, n)),
            scratch_shapes=[pltpu.VMEM((FF, tn), BF16)]),
        compiler_params=_cparams(("arbitrary", "arbitrary"), VMEM_LIMIT),
        name="moe_expert_down",
    )(meta, hmid, w_down)


def _combine_kernel(pos_ref, ys_hbm, x_ref, w_ref, mod_ref, o_ref, buf, sem, *, tm, g_idx, n_tok):
    i = pl.program_id(0)

    def issue(tile, slot):
        base = tile * tm

        def body(g, carry):
            for u in range(DMA_UNROLL):
                r = g * DMA_UNROLL + u
                _row_copy(ys_hbm, buf.at[slot, 0], sem.at[slot, 0], pos_ref[base + r], r).start(priority=0)
                _row_copy(ys_hbm, buf.at[slot, 1], sem.at[slot, 1], pos_ref[n_tok + base + r], r).start(priority=1)
            return carry

        lax.fori_loop(0, tm // DMA_UNROLL, body, 0)

    @pl.when(i == 0)
    def _():
        issue(0, 0)

    @pl.when(i + 1 < n_tok // tm)
    def _():
        issue(i + 1, (i + 1) % 2)

    slot = i % 2
    for k in range(2):
        pltpu.make_async_copy(ys_hbm.at[pl.ds(0, tm)], buf.at[slot, k], sem.at[slot, k]).wait()
    w = w_ref[...]
    y = w[:, 0:1] * buf[slot, 0] + w[:, 1:2] * buf[slot, 1]
    o_ref[...] = x_ref[...] + mod_ref[g_idx:g_idx + 1, :] * y


def _combine_call(pos2, ys, x, w_cols, mod, g_idx, n_prompt, t_s):
    N, D = x.shape
    tm = MOE_TILE
    row = _mod_row_map(tm, n_prompt, t_s)
    return pl.pallas_call(
        functools.partial(_combine_kernel, tm=tm, g_idx=g_idx, n_tok=N),
        out_shape=jax.ShapeDtypeStruct((N, D), F32),
        grid_spec=pltpu.PrefetchScalarGridSpec(
            num_scalar_prefetch=1,
            grid=(N // tm,),
            in_specs=[pl.BlockSpec(memory_space=pl.ANY),
                      pl.BlockSpec((tm, D), lambda i, pos: (i, 0)),
                      pl.BlockSpec((tm, LANES), lambda i, pos: (i, 0)),
                      pl.BlockSpec((None, 6, D), lambda i, pos: (row(i), 0, 0))],
            out_specs=pl.BlockSpec((tm, D), lambda i, pos: (i, 0)),
            scratch_shapes=[pltpu.VMEM((2, 2, tm, D), F32), pltpu.SemaphoreType.DMA((2, 2))]),
        compiler_params=_cparams(("arbitrary",), VMEM_LIMIT),
        name="moe_combine",
    )(pos2, ys, x, w_cols, mod)


def _moe_plan(idx, n_tiles):
    N = idx.shape[1]
    tm = MOE_TILE
    blk = 128
    e_flat = idx.reshape(-1)
    experts = jnp.arange(N_EXPERTS, dtype=jnp.int32)
    onehot = (e_flat[:, None] == experts[None, :]).astype(F32)
    oh3 = onehot.reshape(-1, blk, N_EXPERTS)
    tri = (jnp.arange(blk)[:, None] > jnp.arange(blk)[None, :]).astype(F32)
    within = jnp.einsum("ij,bjk->bik", tri, oh3, precision=lax.Precision.HIGHEST)
    blk_tot = jnp.sum(oh3, axis=1)
    before = jnp.cumsum(blk_tot, axis=0) - blk_tot
    rank = jnp.sum((within + before[:, None, :]) * oh3, axis=-1).reshape(-1).astype(jnp.int32)
    counts = jnp.sum(blk_tot, axis=0).astype(jnp.int32)
    tiles = (counts + tm - 1) // tm
    tile_end = jnp.cumsum(tiles)
    tile_start = tile_end - tiles
    start_of = jnp.sum(onehot.astype(jnp.int32) * tile_start[None, :], axis=1)
    pos = (start_of * tm + rank).astype(jnp.int32)
    tok = jnp.tile(jnp.arange(N, dtype=jnp.int32), 2)
    row_tok = jnp.zeros((n_tiles * tm,), jnp.int32).at[pos].set(tok)
    n_used = tile_end[-1]
    t = jnp.arange(n_tiles, dtype=jnp.int32)
    tile_exp = jnp.sum((tile_end[None, :] <= t[:, None]).astype(jnp.int32), axis=1)
    e_last = jnp.max(jnp.where(tiles > 0, experts, 0))
    tile_exp = jnp.where(t < n_used, tile_exp, e_last)
    meta = jnp.concatenate([tile_exp, n_used[None]]).astype(jnp.int32)
    return row_tok, meta, pos


def _moe(h2f, x, mod, router_wt, router_bias2, w_gate, w_up, w_down, l, n_prompt, t_s):
    N = x.shape[0]
    idx8, wgt8 = _router_call(h2f, router_wt, router_bias2)
    n_tiles = (2 * N) // MOE_TILE + N_EXPERTS
    row_tok, meta, pos2 = _moe_plan(idx8[:2], n_tiles)
    xs = _gather_call(row_tok, meta, h2f, n_tiles)
    hmid = _expert_up_call(meta, xs, w_gate, w_up, l, n_tiles)
    ys = _expert_down_call(meta, hmid, w_down, l, n_tiles)
    w_cols = jnp.zeros((N, LANES), F32).at[:, 0].set(wgt8[0]).at[:, 1].set(wgt8[1])
    return _combine_call(pos2, ys, x, w_cols, mod, 5, n_prompt, t_s)


def _rope_tables(T):
    nf = RET_D // 4
    t = jnp.arange(T)
    pos = jnp.stack([t // GRID_W, t % GRID_W], -1).astype(F32)
    inv = ROPE_BASE ** (-jnp.arange(nf, dtype=F32) / nf)
    ang = pos[:, :, None] * inv
    cos, sin = jnp.cos(ang), jnp.sin(ang)
    cos_t = jnp.concatenate([cos, cos], axis=-1).reshape(T, RET_D)
    sin_t = jnp.concatenate([-sin, sin], axis=-1).reshape(T, RET_D)
    return cos_t, sin_t


def kernel(x_prompt, x_sample, state_ret_fwd, state_ret_bwd, cache_na_k, cache_na_v, state_rwkv_fwd, state_rwkv_bwd, c, c_ctx, w_mod, b_mod, norm_mix, norm_ffn, w_in, w_gate, b_gate, w_br_ret, w_br_na, w_br_rw, w_out, ret_log_decay, ret_gn, na_qn, na_kn, na_rpb, rw_shift, rw_w0, rw_w1, rw_w2, rw_a0, rw_a1, rw_a2, rw_g1, rw_g2, rw_kk, rw_ka, rw_rk, rw_gn, router_w, router_bias, moe_w_gate, moe_w_up, moe_w_down):
    BP, TP, D = x_prompt.shape
    BS, TS, _ = x_sample.shape
    depth = w_in.shape[0]
    NP, NS = BP * TP, BS * TS
    N = NP + NS
    assert NP % (2 * ROW_TILE) == 0 and TS % (2 * ROW_TILE) == 0 and TP % RW_CHUNK == 0 and TS % RW_CHUNK == 0
    assert TS % TP == 0 and NP % TS == 0 and BS <= 7

    x = jnp.concatenate([x_prompt.reshape(NP, D), x_sample.reshape(NS, D)], axis=0)
    c8 = jnp.concatenate([c_ctx[None, :], c, jnp.zeros((7 - BS, D), F32)], axis=0)
    b_mod3 = b_mod.reshape(depth, 1, 6 * D)
    norm_mix3 = norm_mix.reshape(depth, 1, D)
    norm_ffn3 = norm_ffn.reshape(depth, 1, D)
    b_gate3 = b_gate.reshape(depth, 1, 3 * D)
    ret_gn3 = ret_gn.reshape(depth, 1, RET_HEADS * RET_D)
    na_qn3 = na_qn.reshape(depth, 1, NA_DH)
    na_kn3 = na_kn.reshape(depth, 1, NA_DH)
    rw_kk3 = rw_kk.reshape(depth, 1, RW_W)
    rw_ka3 = rw_ka.reshape(depth, 1, RW_W)
    rw_rk3 = rw_rk.reshape(depth, 1, RW_W)
    rw_gn3 = rw_gn.reshape(depth, 1, RW_W)
    zpad = jnp.zeros((depth, D, LANES - 64), F32)
    w_lora = jnp.concatenate([rw_w1[:, 0], zpad, rw_w1[:, 1], zpad, rw_a1[:, 0], zpad, rw_a1[:, 1], zpad,
                              rw_g1], axis=-1)
    kpad = jnp.zeros((depth, 2, LANES - 64, RW_W), F32)
    w2p = jnp.concatenate([rw_w2, kpad], axis=2)
    a2p = jnp.concatenate([rw_a2, kpad], axis=2)
    router_wt = router_w.T
    router_bias2 = router_bias.reshape(N_EXPERTS, 1)
    cos_t, sin_t = _rope_tables(TS)
    zero_st = jnp.zeros((BP, RW_HEADS // 2, LANES, LANES), F32)

    ret_f = jnp.zeros((BP, depth, RET_HEADS, RET_D, RET_D), F32)
    ret_b = jnp.zeros((BP, depth, RET_HEADS, RET_D, RET_D), F32)
    na_k = jnp.zeros((BP, depth, NA_HEADS, TP, NA_DH), F32)
    na_v = jnp.zeros((BP, depth, NA_HEADS, TP, NA_DH), F32)
    rw_f, rw_b = [], []
    for l in range(depth):
        mod = _mod_call(c8, w_mod, b_mod3, l)[:1 + BS].reshape(1 + BS, 6, D)
        (h,) = _norm_mod_call(x, norm_mix3, mod, l, 0, 1, NP, TS, (BF16,))
        proj = _mm_call(h, w_in, l, 2 * ROW_TILE, 512, "in_proj")
        hl = _mm_call(h, w_lora, l, ROW_TILE, LORA_W, "lora_proj")

        y_zero = jnp.zeros((N, RW_W), BF16)
        y_ret, ret_f, ret_b = _ret_call(proj, ret_log_decay[l], ret_gn3, l, BP, TP, 0, False,
                                        (y_zero, ret_f, ret_b))
        y_ret = _ret_call(proj, ret_log_decay[l], ret_gn3, l, BS, TS, NP, True,
                          (cos_t, sin_t, state_ret_fwd, state_ret_bwd, y_ret))
        y_na, na_k, na_v = _na_ctx_call(proj, na_qn3, na_kn3, y_zero, na_k, na_v, l, BP, TP)
        bias = _na_bias(na_rpb[l], TS // GRID_W)
        y_na = _na_lat_call(proj, cache_na_k, cache_na_v, bias, na_qn3, na_kn3, y_na, l, BS, TS, NP)
        y_rw = y_zero
        for (B, T, row0, s0f, s0b) in ((BP, TP, 0, zero_st, zero_st),
                                       (BS, TS, NP, _blockdiag_states(state_rwkv_fwd[:, l]),
                                        _blockdiag_states(state_rwkv_bwd[:, l]))):
            (r, v, al, g, bonus, lw0, be0, kd0, lw1, be1, kd1) = _rw_prep_call(
                proj, hl, rw_shift, w2p, a2p, rw_g2, rw_w0, rw_a0, rw_kk3, rw_ka3, rw_rk3, l, B, T, row0)
            yf, stf = _rw_scan_call(r, v, al, lw0, be0, kd0, s0f, B, T, False)
            yb, stb = _rw_scan_call(r, v, al, lw1, be1, kd1, s0b, B, T, True)
            y_rw = _rw_post_call(yf, yb, bonus, g, rw_gn3, y_rw, l, row0)
            if row0 == 0:
                rw_f.append(_unblock_states(stf))
                rw_b.append(_unblock_states(stb))

        merged = _merge_call(h, y_ret, y_na, y_rw, w_gate, b_gate3, w_br_ret, w_br_na, w_br_rw, l)
        x = _out_call(merged, w_out, x, mod, l, 2, NP, TS)
        (h2f,) = _norm_mod_call(x, norm_ffn3, mod, l, 3, 4, NP, TS, (F32,))
        x = _moe(h2f, x, mod, router_wt, router_bias2, moe_w_gate, moe_w_up, moe_w_down, l, NP, TS)

    y_p = x[:NP].reshape(BP, TP, D)
    y_s = x[NP:].reshape(BS, TS, D)
    return (y_p, y_s, ret_f, ret_b, na_k, na_v, jnp.stack(rw_f, axis=1), jnp.stack(rw_b, axis=1))
```

```python
import functools

import jax
import jax.numpy as jnp
import numpy as np
from jax import lax
from jax.experimental import pallas as pl
from jax.experimental.pallas import tpu as pltpu

F32 = jnp.float32
BF16 = jnp.bfloat16

GRID_W = 64
RET_HEADS = 8
RET_D = 128
NA_HEADS = 8
NA_DH = 128
NA_ROWS = 8
NA_COLS = 16
RW_HEADS = 16
RW_N = 64
RW_W = RW_HEADS * RW_N
N_EXPERTS = 16
N_GROUPS = 4
EXPERTS_PER_GROUP = 4
ROPE_BASE = 10000.0
NORM_EPS = 1e-6
NEG_INF = -1e30

LANES = 128
HEAD_BLOCK = 2
RW_CHUNK = 64
RW_CB = 256
LORA_W = 640
ROW_TILE = 512
MOE_TILE = 256
DMA_UNROLL = 8
VMEM_LIMIT = 56 * 1024 * 1024


def _cparams(sem, vmem=None):
    return pltpu.CompilerParams(dimension_semantics=sem, vmem_limit_bytes=vmem)


def _dot(a, b):
    return jnp.dot(a.astype(BF16), b.astype(BF16), preferred_element_type=F32)


def _dot_nt(a, b):
    return lax.dot_general(a.astype(BF16), b.astype(BF16), (((1,), (1,)), ((), ())),
                           preferred_element_type=F32)


def _dot_tn(a, b):
    return lax.dot_general(a.astype(BF16), b.astype(BF16), (((0,), (0,)), ((), ())),
                           preferred_element_type=F32)


def _split3(x):
    h = x.astype(BF16)
    r1 = x - h.astype(F32)
    m = r1.astype(BF16)
    l = (r1 - m.astype(F32)).astype(BF16)
    return h, m, l


def _dot_exact_lhs(c, x):
    h, m, l = _split3(x)
    cb = c.astype(BF16)
    return (jnp.dot(cb, h, preferred_element_type=F32) + jnp.dot(cb, m, preferred_element_type=F32)
            + jnp.dot(cb, l, preferred_element_type=F32))


def _dot_exact_rhs(x, c):
    h, m, l = _split3(x)
    cb = c.astype(BF16)
    return (jnp.dot(h, cb, preferred_element_type=F32) + jnp.dot(m, cb, preferred_element_type=F32)
            + jnp.dot(l, cb, preferred_element_type=F32))


def _head_sum_matrix():
    r = lax.broadcasted_iota(jnp.int32, (LANES, LANES), 0) // RW_N
    c = lax.broadcasted_iota(jnp.int32, (LANES, LANES), 1) // RW_N
    return (r == c).astype(F32)


def _seg_sum(x, p):
    parts = [_dot_exact_rhs(x[:, s * LANES:(s + 1) * LANES], p) for s in range(x.shape[1] // LANES)]
    return parts[0] if len(parts) == 1 else jnp.concatenate(parts, axis=1)


def _sigmoid(x):
    return 1.0 / (1.0 + jnp.exp(-x))


def _silu(x):
    return x * _sigmoid(x)


def _pack_bf16_pairs(x):
    k = x.shape[1] // 2
    first = lax.bitcast_convert_type(x[:, :k].astype(BF16).astype(F32), jnp.uint32)
    second = lax.bitcast_convert_type(x[:, k:].astype(BF16).astype(F32), jnp.uint32)
    return first | (second >> 16)


def _unpack_bf16_pairs(p):
    first = lax.bitcast_convert_type(p & jnp.uint32(0xFFFF0000), F32)
    second = lax.bitcast_convert_type(p << 16, F32)
    return first, second


def _mod_kernel(c_ref, w_ref, b_ref, o_ref):
    a = _silu(c_ref[...])
    o_ref[...] = _dot(a, w_ref[...]) + b_ref[...]


def _mod_call(c8, w_mod, b_mod3, l):
    D = c8.shape[1]
    n_out = w_mod.shape[2]
    tn = 512
    return pl.pallas_call(
        _mod_kernel,
        out_shape=jax.ShapeDtypeStruct((8, n_out), F32),
        grid=(n_out // tn,),
        in_specs=[pl.BlockSpec((8, D), lambda j: (0, 0)),
                  pl.BlockSpec((None, D, tn), lambda j: (l, 0, j)),
                  pl.BlockSpec((None, 1, tn), lambda j: (l, 0, j))],
        out_specs=pl.BlockSpec((8, tn), lambda j: (0, j)),
        compiler_params=_cparams(("arbitrary",), VMEM_LIMIT),
        name="mod_proj",
    )(c8, w_mod, b_mod3)


def _norm_mod_kernel(x_ref, g_ref, mod_ref, *out_refs, sh_idx, sc_idx):
    x = x_ref[...]
    y = x * lax.rsqrt(jnp.mean(x * x, -1, keepdims=True) + NORM_EPS) * g_ref[...]
    h = y * (1.0 + mod_ref[sc_idx:sc_idx + 1, :]) + mod_ref[sh_idx:sh_idx + 1, :]
    for o in out_refs:
        o[...] = _pack_bf16_pairs(h) if o.dtype == jnp.uint32 else h.astype(o.dtype)


def _mod_row_map(tm, n_prompt, t_s):
    def row(i):
        r = i * tm
        return jnp.where(r < n_prompt, 0, 1 + (r - n_prompt) // t_s)
    return row


def _norm_mod_call(x, g3, mod, l, sh_idx, sc_idx, n_prompt, t_s, out_dtypes):
    N, D = x.shape
    tm = 256
    row = _mod_row_map(tm, n_prompt, t_s)
    width = lambda dt: D // 2 if dt == jnp.uint32 else D
    outs = tuple(jax.ShapeDtypeStruct((N, width(dt)), dt) for dt in out_dtypes)
    return pl.pallas_call(
        functools.partial(_norm_mod_kernel, sh_idx=sh_idx, sc_idx=sc_idx),
        out_shape=outs,
        grid=(N // tm,),
        in_specs=[pl.BlockSpec((tm, D), lambda i: (i, 0)),
                  pl.BlockSpec((None, 1, D), lambda i: (l, 0, 0)),
                  pl.BlockSpec((None, 6, D), lambda i: (row(i), 0, 0))],
        out_specs=tuple(pl.BlockSpec((tm, width(dt)), lambda i: (i, 0)) for dt in out_dtypes),
        compiler_params=_cparams(("arbitrary",), VMEM_LIMIT),
        name="norm_mod",
    )(x, g3, mod)


def _mm_kernel(a_ref, w_ref, o_ref, wbf_ref):
    @pl.when(pl.program_id(1) == 0)
    def _():
        wbf_ref[...] = w_ref[...].astype(BF16)

    o_ref[...] = jnp.dot(a_ref[...], wbf_ref[...], preferred_element_type=F32).astype(o_ref.dtype)


def _mm_call(a, w, l, tm, tn, name, out_dtype=F32):
    N, K = a.shape
    M = w.shape[-1]
    if w.ndim == 3:
        w_spec = pl.BlockSpec((None, K, tn), lambda j, i: (l, 0, j))
    else:
        w_spec = pl.BlockSpec((K, tn), lambda j, i: (0, j))
    return pl.pallas_call(
        _mm_kernel,
        out_shape=jax.ShapeDtypeStruct((N, M), out_dtype),
        grid=(M // tn, N // tm),
        in_specs=[pl.BlockSpec((tm, K), lambda j, i: (i, 0)), w_spec],
        out_specs=pl.BlockSpec((tm, tn), lambda j, i: (i, j)),
        scratch_shapes=[pltpu.VMEM((K, tn), BF16)],
        compiler_params=_cparams(("arbitrary", "arbitrary"), VMEM_LIMIT),
        name=name,
    )(a, w)


def _rope(x, cos, sin):
    lane = lax.broadcasted_iota(jnp.int32, x.shape, 1)
    first = (lane % 64) < 32
    rot = jnp.where(first, pltpu.roll(x, LANES - 32, 1), pltpu.roll(x, 32, 1))
    return x * cos + rot * sin


def _ret_kernel(lg_ref, q_ref, k_ref, v_ref, g_ref, gn_ref, *rest, T, tq, latent):
    if latent:
        cosq_ref, sinq_ref, cosk_ref, sink_ref, s0f_ref, s0b_ref, _, y_ref = rest
    else:
        y_ref, sf_ref, sb_ref = rest[-3:]
    hp = pl.program_id(1)
    qi = pl.program_id(2)
    hs = range(HEAD_BLOCK)
    sl = [slice(j * RET_D, (j + 1) * RET_D) for j in hs]
    lgf = [lg_ref[0, hp * HEAD_BLOCK + j] for j in hs]
    lgb = [lg_ref[1, hp * HEAD_BLOCK + j] for j in hs]
    q = [q_ref[:, s] for s in sl]
    k = [k_ref[:, s] for s in sl]
    v = [v_ref[:, s].astype(BF16) for s in sl]
    if latent:
        q = [_rope(x, cosq_ref[...], sinq_ref[...]) for x in q]
        k = [_rope(x, cosk_ref[...], sink_ref[...]) for x in k]
    k = [x * (RET_D ** -0.5) for x in k]
    t_idx = (lax.broadcasted_iota(jnp.int32, (tq, T), 0) + qi * tq).astype(F32)
    s_idx = lax.broadcasted_iota(jnp.int32, (tq, T), 1).astype(F32)
    diff = t_idx - s_idx
    fwd, bwd = jnp.maximum(diff, 0.0), jnp.maximum(-diff, 0.0)
    dmask = [jnp.where(diff >= 0, jnp.exp(lgf[j] * fwd), 0.0) + jnp.where(diff <= 0, jnp.exp(lgb[j] * bwd), 0.0)
             for j in hs]
    scores = [_dot_nt(q[j], k[j]) * dmask[j] for j in hs]
    y = [_dot(scores[j], v[j]) for j in hs]
    if latent:
        tq_col = (lax.broadcasted_iota(jnp.int32, (tq, 1), 0) + qi * tq).astype(F32)
        y = [y[j] + _dot(q[j] * jnp.exp(lgf[j] * (tq_col + 1.0)), s0f_ref[j])
             + _dot(q[j] * jnp.exp(lgb[j] * (T - tq_col)), s0b_ref[j]) for j in hs]
    else:
        s_col = lax.broadcasted_iota(jnp.int32, (T, 1), 0).astype(F32)
        for j in hs:
            sf_ref[j] = _dot_tn(k[j] * jnp.exp(lgf[j] * (T - 1.0 - s_col)), v[j])
            sb_ref[j] = _dot_tn(k[j] * jnp.exp(lgb[j] * s_col), v[j])
    for j in hs:
        mu = jnp.mean(y[j], -1, keepdims=True)
        yc = y[j] - mu
        var = jnp.mean(yc * yc, -1, keepdims=True)
        yn = yc * lax.rsqrt(var + 1e-5) * gn_ref[:, sl[j]]
        y_ref[:, sl[j]] = (yn * _silu(g_ref[:, sl[j]])).astype(y_ref.dtype)


def _ret_call(proj, lg, gn3, l, B, T, row0, latent, extra):
    N = proj.shape[0]
    tq = 256
    nq = T // tq
    rb = row0 // T
    HB = HEAD_BLOCK
    nh = RET_HEADS // HB
    bw = HB * RET_D
    qmap = lambda b, h, qi: (rb * nq + b * nq + qi, h)
    kmap = lambda b, h, qi: (rb + b, nh + h)
    vmap = lambda b, h, qi: (rb + b, 2 * nh + h)
    gmap = lambda b, h, qi: (rb * nq + b * nq + qi, 3 * nh + h)
    in_specs = [pl.BlockSpec(memory_space=pltpu.SMEM),
                pl.BlockSpec((tq, bw), qmap),
                pl.BlockSpec((T, bw), kmap),
                pl.BlockSpec((T, bw), vmap),
                pl.BlockSpec((tq, bw), gmap),
                pl.BlockSpec((None, 1, bw), lambda b, h, qi: (l, 0, h))]
    args = [lg, proj, proj, proj, proj, gn3]
    y_spec = pl.BlockSpec((tq, bw), qmap)
    y_shape = jax.ShapeDtypeStruct((N, RET_HEADS * RET_D), BF16)
    st_blk = pl.BlockSpec((None, None, HB, RET_D, RET_D), lambda b, h, qi: (b, l, h, 0, 0))
    if latent:
        cos, sin, s0f, s0b, y_buf = extra
        in_specs += [pl.BlockSpec((tq, RET_D), lambda b, h, qi: (qi, 0)),
                     pl.BlockSpec((tq, RET_D), lambda b, h, qi: (qi, 0)),
                     pl.BlockSpec((T, RET_D), lambda b, h, qi: (0, 0)),
                     pl.BlockSpec((T, RET_D), lambda b, h, qi: (0, 0)),
                     st_blk, st_blk, pl.BlockSpec(memory_space=pl.ANY)]
        args += [cos, sin, cos, sin, s0f, s0b, y_buf]
        aliases = {len(args) - 1: 0}
        out_shape = y_shape
        out_specs = y_spec
    else:
        y_buf, sf_buf, sb_buf = extra
        st = jax.ShapeDtypeStruct(sf_buf.shape, F32)
        st_spec = st_blk
        in_specs += [pl.BlockSpec(memory_space=pl.ANY)] * 3
        args += [y_buf, sf_buf, sb_buf]
        aliases = {len(args) - 3: 0, len(args) - 2: 1, len(args) - 1: 2}
        out_shape = (y_shape, st, st)
        out_specs = (y_spec, st_spec, st_spec)
    return pl.pallas_call(
        functools.partial(_ret_kernel, T=T, tq=tq, latent=latent),
        out_shape=out_shape,
        grid=(B, nh, nq),
        in_specs=in_specs,
        out_specs=out_specs,
        input_output_aliases=aliases,
        compiler_params=_cparams(("arbitrary", "arbitrary", "arbitrary"), VMEM_LIMIT),
        name="retention_latent" if latent else "retention_ctx",
    )(*args)


def _rms(x, g):
    return x * lax.rsqrt(jnp.mean(x * x, -1, keepdims=True) + NORM_EPS) * g


def _na_ctx_kernel(q_ref, k_ref, v_ref, qn_ref, kn_ref, *rest):
    y_ref, ko_ref, vo_ref = rest[-3:]
    hs = range(HEAD_BLOCK)
    sl = [slice(j * NA_DH, (j + 1) * NA_DH) for j in hs]
    q = [_rms(q_ref[:, s], qn_ref[...]) * (NA_DH ** -0.5) for s in sl]
    k = [_rms(k_ref[:, s], kn_ref[...]) for s in sl]
    v = [v_ref[:, s] for s in sl]
    s = [_dot_nt(q[j], k[j]) for j in hs]
    p = [jnp.exp(x - jnp.max(x, -1, keepdims=True)) for x in s]
    o = [_dot(p[j], v[j]) for j in hs]
    for j in hs:
        y_ref[:, sl[j]] = (o[j] / jnp.sum(p[j], -1, keepdims=True)).astype(y_ref.dtype)
        ko_ref[j] = k[j]
        vo_ref[j] = v[j]


def _na_ctx_call(proj, qn3, kn3, y_buf, k_buf, v_buf, l, B, T):
    N = proj.shape[0]
    HB = HEAD_BLOCK
    nh = NA_HEADS // HB
    bw = HB * NA_DH
    c0 = 4 * (RET_HEADS // HB)
    kv = jax.ShapeDtypeStruct(k_buf.shape, F32)
    kv_spec = pl.BlockSpec((None, None, HB, T, NA_DH), lambda b, h: (b, l, h, 0, 0))
    any_spec = pl.BlockSpec(memory_space=pl.ANY)
    return pl.pallas_call(
        _na_ctx_kernel,
        out_shape=(jax.ShapeDtypeStruct((N, NA_HEADS * NA_DH), BF16), kv, kv),
        grid=(B, nh),
        in_specs=[pl.BlockSpec((T, bw), lambda b, h: (b, c0 + h)),
                  pl.BlockSpec((T, bw), lambda b, h: (b, c0 + nh + h)),
                  pl.BlockSpec((T, bw), lambda b, h: (b, c0 + 2 * nh + h)),
                  pl.BlockSpec((None, 1, NA_DH), lambda b, h: (l, 0, 0)),
                  pl.BlockSpec((None, 1, NA_DH), lambda b, h: (l, 0, 0)),
                  any_spec, any_spec, any_spec],
        out_specs=(pl.BlockSpec((T, bw), lambda b, h: (b, h)), kv_spec, kv_spec),
        input_output_aliases={5: 0, 6: 1, 7: 2},
        compiler_params=_cparams(("arbitrary", "arbitrary"), VMEM_LIMIT),
        name="na_ctx",
    )(proj, proj, proj, qn3, kn3, y_buf, k_buf, v_buf)


def _na_lat_kernel(q_ref, k_ref, v_ref, ck_ref, cv_ref, bias_ref, qn_ref, kn_ref, _, y_ref, *, rows, kr):
    q = _rms(q_ref[...], qn_ref[...]) * (NA_DH ** -0.5)
    k = _rms(k_ref[...], kn_ref[...]).astype(BF16)
    v = v_ref[...].astype(BF16)
    ck = ck_ref[...].astype(BF16)
    cv = cv_ref[...].astype(BF16)
    nl = kr * GRID_W
    qb = q.astype(BF16)
    starts = [min(max(r - kr // 2, 0), rows - kr) for r in range(rows)]
    groups = []
    for r, rs in enumerate(starts):
        if groups and groups[-1][0] == rs:
            groups[-1][2] = r + 1
        else:
            groups.append([rs, r, r + 1])
    W = GRID_W
    s_ctx = _dot_nt(qb, ck)
    s_loc = [_dot_nt(qb[r0 * W:r1 * W], k[rs * W:rs * W + nl])
             + bias_ref[r0:r1].reshape((r1 - r0) * W, nl) for rs, r0, r1 in groups]
    p_loc, p_ctx, den = [], [], []
    for (rs, r0, r1), sl in zip(groups, s_loc):
        sc = s_ctx[r0 * W:r1 * W]
        m = jnp.maximum(jnp.max(sl, -1, keepdims=True), jnp.max(sc, -1, keepdims=True))
        pl_, pc_ = jnp.exp(sl - m), jnp.exp(sc - m)
        p_loc.append(pl_.astype(BF16))
        p_ctx.append(pc_.astype(BF16))
        den.append(jnp.sum(pl_, -1, keepdims=True) + jnp.sum(pc_, -1, keepdims=True))
    o_ctx = _dot(jnp.concatenate(p_ctx, axis=0), cv)
    o_loc = [_dot(p, v[rs * W:rs * W + nl]) for (rs, r0, r1), p in zip(groups, p_loc)]
    for (rs, r0, r1), ol, d in zip(groups, o_loc, den):
        y_ref[r0 * W:r1 * W, :] = ((ol + o_ctx[r0 * W:r1 * W]) / d).astype(y_ref.dtype)


def _na_bias(rpb, rows):
    H = rpb.shape[0]
    kr = min(NA_ROWS, rows)
    r_ids = np.arange(rows)
    row_start = np.clip(r_ids - kr // 2, 0, rows - kr)
    row_idx = row_start[:, None] + np.arange(kr)[None, :]
    cols = np.arange(GRID_W)
    col_start = np.clip(cols - NA_COLS // 2, 0, GRID_W - NA_COLS)
    col_ok = (cols[None, :] >= col_start[:, None]) & (cols[None, :] < col_start[:, None] + NA_COLS)
    dr_idx = row_idx - r_ids[:, None] + (NA_ROWS - 1)
    dc_idx = np.clip(cols[None, :] - cols[:, None] + (NA_COLS - 1), 0, 2 * NA_COLS - 2)
    rpb = rpb.astype(F32)
    blk = jnp.zeros((H, 2 * NA_ROWS - 1, GRID_W, GRID_W), F32)
    for d in range(2 * NA_COLS - 1):
        blk = jnp.where(jnp.asarray(dc_idx == d)[None, None], rpb[:, :, d][:, :, None, None], blk)
    blk = jnp.where(jnp.asarray(col_ok)[None, None], blk, NEG_INF)
    rows_out = [jnp.concatenate([blk[:, int(dr_idx[r, j])] for j in range(kr)], axis=-1) for r in range(rows)]
    return jnp.stack(rows_out, axis=1)


def _na_lat_call(proj, cache_k, cache_v, bias, qn3, kn3, y_prev, l, B, T, row0):
    N = proj.shape[0]
    c0 = 4 * RET_HEADS
    rb = row0 // T
    rows = T // GRID_W
    kr = min(NA_ROWS, rows)
    L = cache_k.shape[3]
    return pl.pallas_call(
        functools.partial(_na_lat_kernel, rows=rows, kr=kr),
        out_shape=jax.ShapeDtypeStruct((N, NA_HEADS * NA_DH), BF16),
        grid=(B, NA_HEADS),
        in_specs=[pl.BlockSpec((T, NA_DH), lambda b, h: (rb + b, c0 + h)),
                  pl.BlockSpec((T, NA_DH), lambda b, h: (rb + b, c0 + NA_HEADS + h)),
                  pl.BlockSpec((T, NA_DH), lambda b, h: (rb + b, c0 + 2 * NA_HEADS + h)),
                  pl.BlockSpec((None, None, None, L, NA_DH), lambda b, h: (b, l, h, 0, 0)),
                  pl.BlockSpec((None, None, None, L, NA_DH), lambda b, h: (b, l, h, 0, 0)),
                  pl.BlockSpec((None, rows, GRID_W, kr * GRID_W), lambda b, h: (h, 0, 0, 0)),
                  pl.BlockSpec((None, 1, NA_DH), lambda b, h: (l, 0, 0)),
                  pl.BlockSpec((None, 1, NA_DH), lambda b, h: (l, 0, 0)),
                  pl.BlockSpec(memory_space=pl.ANY)],
        out_specs=pl.BlockSpec((T, NA_DH), lambda b, h: (rb + b, h)),
        input_output_aliases={8: 0},
        compiler_params=_cparams(("arbitrary", "arbitrary"), VMEM_LIMIT),
        name="na_latent",
    )(proj, proj, proj, cache_k, cache_v, bias, qn3, kn3, y_prev)


def _rw_prep_kernel(xr_ref, xk_ref, xv_ref, shr_ref, shk_ref, shv_ref, hl_ref, w2_ref, a2_ref, g2_ref,
                    w0_ref, a0_ref, kk_ref, ka_ref, rk_ref,
                    r_o, v_o, al_o, g_o, bon_o, lw0_o, be0_o, kd0_o, lw1_o, be1_o, kd1_o, *, T):
    row = lax.broadcasted_iota(jnp.int32, (T, 1), 0)

    def shift(x_ref, sh_ref):
        x = x_ref[...]
        prev = jnp.where(row == 0, 0.0, pltpu.roll(x, 1, 0))
        nxt = jnp.where(row == T - 1, 0.0, pltpu.roll(x, T - 1, 0))
        return sh_ref[0:1, :] * prev + sh_ref[1:2, :] * x + sh_ref[2:3, :] * nxt

    r = shift(xr_ref, shr_ref)
    k = shift(xk_ref, shk_ref)
    v = shift(xv_ref, shv_ref)
    p = _head_sum_matrix()
    hl = hl_ref[...]
    g = _dot(_sigmoid(hl[:, 4 * LANES:5 * LANES]), g2_ref[...])
    kk = k * kk_ref[...]
    kk = kk / jnp.maximum(jnp.sqrt(_seg_sum(kk * kk, p)), 1e-6)
    bonus = _seg_sum(r * k * rk_ref[...], p) * v
    r_o[...] = r
    v_o[...] = v
    al_o[...] = -kk
    g_o[...] = g
    bon_o[...] = bonus
    outs = ((lw0_o, be0_o, kd0_o), (lw1_o, be1_o, kd1_o))
    for d in range(2):
        z = w0_ref[d:d + 1, :] + _dot(jnp.tanh(hl[:, d * LANES:(d + 1) * LANES]), w2_ref[d])
        u = -z
        softplus = jnp.maximum(u, 0.0) + jnp.log(1.0 + jnp.exp(-jnp.abs(u)))
        wlog = -softplus - 0.5
        a = _sigmoid(a0_ref[d:d + 1, :] + _dot(hl[:, (2 + d) * LANES:(3 + d) * LANES], a2_ref[d]))
        lw_o, be_o, kd_o = outs[d]
        lw_o[...] = -jnp.exp(wlog)
        be_o[...] = kk * a
        kd_o[...] = k * (1.0 + (a - 1.0) * ka_ref[...])


def _rw_prep_call(proj, hl, shift3, w2p, a2p, g2, w0, a0, kk3, ka3, rk3, l, B, T, row0):
    rb = row0 // T
    nc = RW_W // RW_CB
    c0 = 7 * (RW_W // RW_CB)
    n_rows = B * T
    big = lambda off: pl.BlockSpec((T, RW_CB), lambda b, c: (rb + b, c0 + off * nc + c))
    sh = lambda off: pl.BlockSpec((None, 3, RW_CB), lambda b, c: (l, 0, off * nc + c))
    vec = pl.BlockSpec((None, 1, RW_CB), lambda b, c: (l, 0, c))
    out_spec = pl.BlockSpec((T, RW_CB), lambda b, c: (b, c))
    out = jax.ShapeDtypeStruct((n_rows, RW_W), F32)
    return pl.pallas_call(
        functools.partial(_rw_prep_kernel, T=T),
        out_shape=(out,) * 11,
        grid=(B, nc),
        in_specs=[big(0), big(1), big(2), sh(0), sh(1), sh(2),
                  pl.BlockSpec((T, LORA_W), lambda b, c: (rb + b, 0)),
                  pl.BlockSpec((None, 2, LANES, RW_CB), lambda b, c: (l, 0, 0, c)),
                  pl.BlockSpec((None, 2, LANES, RW_CB), lambda b, c: (l, 0, 0, c)),
                  pl.BlockSpec((None, LANES, RW_CB), lambda b, c: (l, 0, c)),
                  pl.BlockSpec((None, 2, RW_CB), lambda b, c: (l, 0, c)),
                  pl.BlockSpec((None, 2, RW_CB), lambda b, c: (l, 0, c)),
                  vec, vec, vec],
        out_specs=(out_spec,) * 11,
        compiler_params=_cparams(("arbitrary", "arbitrary"), VMEM_LIMIT),
        name="rwkv_prep",
    )(proj, proj, proj, shift3, shift3, shift3, hl, w2p, a2p, g2, w0, a0, kk3, ka3, rk3)


def _rw_scan_kernel(r_ref, v_ref, al_ref, lw_ref, be_ref, kd_ref, s0_ref, y_ref, sf_ref, st_ref,
                    *, reverse, nchunks):
    C = RW_CHUNK
    c = pl.program_id(1)

    @pl.when(c == 0)
    def _():
        st_ref[...] = s0_ref[...]

    ti = lax.broadcasted_iota(jnp.int32, (C, C), 0)
    si = lax.broadcasted_iota(jnp.int32, (C, C), 1)
    incl = (ti <= si) if reverse else (ti >= si)
    lw = lw_ref[...]
    cum = _dot_exact_lhs(incl.astype(F32), lw)
    tot = jnp.sum(lw, axis=0, keepdims=True)
    e_incl = jnp.exp(cum)
    e_excl = jnp.exp(cum - lw)
    e_neg = jnp.exp(-cum)
    e_rem = jnp.exp(tot - cum)
    g_tot = jnp.exp(tot)
    be = be_ref[...]
    kd = kd_ref[...]
    scaled = (al_ref[...] * e_excl, r_ref[...] * e_incl, be * e_neg, kd * e_neg, v_ref[...],
              be * e_rem, kd * e_rem)

    S = 2 * C
    t2 = lax.broadcasted_iota(jnp.int32, (S, S), 0)
    s2 = lax.broadcasted_iota(jnp.int32, (S, S), 1)
    same = (t2 // C) == (s2 // C)
    if reverse:
        strict2 = same & (t2 < s2)
        incl2 = same & (t2 <= s2)
    else:
        strict2 = same & (t2 > s2)
        incl2 = same & (t2 >= s2)
    eye = t2 == s2
    lane = lax.broadcasted_iota(jnp.int32, (C, LANES), 1)
    m0 = (lane < RW_N).astype(F32)
    m1 = 1.0 - m0

    def stack(x):
        return jnp.concatenate([x * m0, x * m1], axis=0).astype(BF16)

    pairs = range(RW_HEADS // 2)
    cat = jnp.concatenate
    sls = [slice(p * LANES, (p + 1) * LANES) for p in pairs]
    a_s, r_s, b_s, k_s, v_s, bg_s, kg_s = ([stack(x[:, sl]) for sl in sls] for x in scaled)
    hst = [st_ref[p] for p in pairs]
    h_b = [h.astype(BF16) for h in hst]
    m_all = [_dot_nt(cat([a_s[p], r_s[p]], 0), cat([b_s[p], k_s[p]], 0)) for p in pairs]
    m_ab = [jnp.where(strict2, m_all[p][:S, :S], 0.0) for p in pairs]
    m_ak = [jnp.where(strict2, m_all[p][:S, S:], 0.0).astype(BF16) for p in pairs]
    m_rb = [jnp.where(incl2, m_all[p][S:, :S], 0.0).astype(BF16) for p in pairs]
    m_rk = [jnp.where(incl2, m_all[p][S:, S:], 0.0).astype(BF16) for p in pairs]
    x = [_dot(cat([a_s[p], m_ak[p]], 1), cat([h_b[p], v_s[p]], 0)) for p in pairs]
    pw = m_ab
    n_steps = int(np.log2(C))
    for step in range(n_steps):
        if step + 1 < n_steps:
            px = [_dot(pw[p], cat([x[p], pw[p]], 1)) for p in pairs]
            x = [x[p] + px[p][:, :S] for p in pairs]
            pw = [px[p][:, S:] for p in pairs]
        else:
            x = [x[p] + _dot(pw[p], x[p]) for p in pairs]
    u_s = [x[p].astype(BF16) for p in pairs]
    y_s = [_dot(cat([r_s[p], m_rb[p], m_rk[p]], 1), cat([h_b[p], u_s[p], v_s[p]], 0)) for p in pairs]
    upd = [_dot_tn(cat([bg_s[p], kg_s[p]], 0), cat([u_s[p], v_s[p]], 0)) for p in pairs]
    for p in pairs:
        y_ref[:, sls[p]] = y_s[p][:C] + y_s[p][C:]
        g_col = jnp.sum(jnp.where(eye, jnp.broadcast_to(g_tot[:, sls[p]], (S, S)), 0.0), axis=1, keepdims=True)
        st_ref[p] = hst[p] * g_col + upd[p]

    @pl.when(c == nchunks - 1)
    def _():
        sf_ref[...] = st_ref[...]


def _rw_scan_call(r, v, al, lw, be, kd, s0, B, T, reverse):
    C = RW_CHUNK
    nch = T // C
    if reverse:
        cmap = lambda b, c: (b * nch + (nch - 1 - c), 0)
    else:
        cmap = lambda b, c: (b * nch + c, 0)
    blk = pl.BlockSpec((C, RW_W), cmap)
    st_spec = pl.BlockSpec((None, RW_HEADS // 2, LANES, LANES), lambda b, c: (b, 0, 0, 0))
    return pl.pallas_call(
        functools.partial(_rw_scan_kernel, reverse=reverse, nchunks=nch),
        out_shape=(jax.ShapeDtypeStruct((B * T, RW_W), F32),
                   jax.ShapeDtypeStruct((B, RW_HEADS // 2, LANES, LANES), F32)),
        grid=(B, nch),
        in_specs=[blk] * 6 + [st_spec],
        out_specs=(blk, st_spec),
        scratch_shapes=[pltpu.VMEM((RW_HEADS // 2, LANES, LANES), F32)],
        compiler_params=_cparams(("arbitrary", "arbitrary"), VMEM_LIMIT),
        name="rwkv_scan_bwd" if reverse else "rwkv_scan_fwd",
    )(r, v, al, lw, be, kd, s0)


def _rw_post_kernel(yf_ref, yb_ref, bon_ref, g_ref, gn_ref, *rest):
    o_ref = rest[-1]
    p = _head_sum_matrix()
    y = yf_ref[...] + yb_ref[...]
    mu = _seg_sum(y, p) * (1.0 / RW_N)
    yc = y - mu
    var = _seg_sum(yc * yc, p) * (1.0 / RW_N)
    yn = yc * lax.rsqrt(var + 64e-5) * gn_ref[...]
    o_ref[...] = ((yn + bon_ref[...]) * g_ref[...]).astype(o_ref.dtype)


def _rw_post_call(yf, yb, bonus, g, gn3, y_buf, l, row0):
    n_rows = yf.shape[0]
    tm = 256
    blk = pl.BlockSpec((tm, RW_W), lambda i: (i, 0))
    return pl.pallas_call(
        _rw_post_kernel,
        out_shape=jax.ShapeDtypeStruct(y_buf.shape, BF16),
        grid=(n_rows // tm,),
        in_specs=[blk, blk, blk, blk, pl.BlockSpec((None, 1, RW_W), lambda i: (l, 0, 0)),
                  pl.BlockSpec(memory_space=pl.ANY)],
        out_specs=pl.BlockSpec((tm, RW_W), lambda i: (row0 // tm + i, 0)),
        input_output_aliases={5: 0},
        compiler_params=_cparams(("arbitrary",), VMEM_LIMIT),
        name="rwkv_post",
    )(yf, yb, bonus, g, gn3, y_buf)


def _blockdiag_states(s):
    B = s.shape[0]
    s = jnp.swapaxes(s, -1, -2).reshape(B, RW_HEADS // 2, 2, RW_N, RW_N)
    z = jnp.zeros_like(s[:, :, 0])
    top = jnp.concatenate([s[:, :, 0], z], axis=-1)
    bot = jnp.concatenate([z, s[:, :, 1]], axis=-1)
    return jnp.concatenate([top, bot], axis=-2)


def _unblock_states(sd):
    B = sd.shape[0]
    h0 = sd[:, :, :RW_N, :RW_N]
    h1 = sd[:, :, RW_N:, RW_N:]
    return jnp.swapaxes(jnp.stack([h0, h1], axis=2).reshape(B, RW_HEADS, RW_N, RW_N), -1, -2)


def _merge_kernel(h_ref, ya_ref, yb_ref, yc_ref, wga_ref, wgb_ref, wgc_ref, bga_ref, bgb_ref, bgc_ref,
                  wa_ref, wb_ref, wc_ref, o_ref, wg_s, wbr_s):
    @pl.when(pl.program_id(1) == 0)
    def _():
        for n, (wg, wb) in enumerate(((wga_ref, wa_ref), (wgb_ref, wb_ref), (wgc_ref, wc_ref))):
            wg_s[n] = wg[...].astype(BF16)
            wbr_s[n] = wb[...].astype(BF16)

    h = h_ref[...]
    acc = None
    for n, (y_ref, bg_ref) in enumerate(((ya_ref, bga_ref), (yb_ref, bgb_ref), (yc_ref, bgc_ref))):
        gate = _sigmoid(jnp.dot(h, wg_s[n], preferred_element_type=F32) + bg_ref[...])
        term = gate * jnp.dot(y_ref[...], wbr_s[n], preferred_element_type=F32)
        acc = term if acc is None else acc + term
    o_ref[...] = acc.astype(o_ref.dtype)


def _merge_call(h, y_ret, y_na, y_rw, w_gate, b_gate3, w_ret, w_na, w_rw, l):
    N, D = h.shape
    KB = y_ret.shape[1]
    tm, tn = ROW_TILE, 256
    nj = D // tn
    a_spec = pl.BlockSpec((tm, D), lambda j, i: (i, 0))
    y_spec = pl.BlockSpec((tm, KB), lambda j, i: (i, 0))
    once = pl.Buffered(1)
    wg = lambda n: pl.BlockSpec((None, D, tn), lambda j, i: (l, 0, n * nj + j))
    bg = lambda n: pl.BlockSpec((None, 1, tn), lambda j, i: (l, 0, n * nj + j))
    wb = pl.BlockSpec((None, KB, tn), lambda j, i: (l, 0, j), pipeline_mode=once)
    return pl.pallas_call(
        _merge_kernel,
        out_shape=jax.ShapeDtypeStruct((N, D), BF16),
        grid=(nj, N // tm),
        in_specs=[a_spec, y_spec, y_spec, y_spec, wg(0), wg(1), wg(2), bg(0), bg(1), bg(2), wb, wb, wb],
        out_specs=pl.BlockSpec((tm, tn), lambda j, i: (i, j)),
        scratch_shapes=[pltpu.VMEM((3, D, tn), BF16), pltpu.VMEM((3, KB, tn), BF16)],
        compiler_params=_cparams(("arbitrary", "arbitrary"), VMEM_LIMIT),
        name="gated_merge",
    )(h, y_ret, y_na, y_rw, w_gate, w_gate, w_gate, b_gate3, b_gate3, b_gate3, w_ret, w_na, w_rw)


def _out_kernel(a_ref, w_ref, x_ref, mod_ref, o_ref, wbf_ref, *, g_idx):
    @pl.when(pl.program_id(1) == 0)
    def _():
        wbf_ref[...] = w_ref[...].astype(BF16)

    m = jnp.dot(a_ref[...], wbf_ref[...], preferred_element_type=F32)
    o_ref[...] = x_ref[...] + mod_ref[g_idx:g_idx + 1, :] * m


def _out_call(a, w_out, x, mod, l, g_idx, n_prompt, t_s):
    N, D = x.shape
    tm, tn = 2 * ROW_TILE, 512
    row = _mod_row_map(tm, n_prompt, t_s)
    return pl.pallas_call(
        functools.partial(_out_kernel, g_idx=g_idx),
        out_shape=jax.ShapeDtypeStruct((N, D), F32),
        grid=(D // tn, N // tm),
        in_specs=[pl.BlockSpec((tm, D), lambda j, i: (i, 0)),
                  pl.BlockSpec((None, D, tn), lambda j, i: (l, 0, j)),
                  pl.BlockSpec((tm, tn), lambda j, i: (i, j)),
                  pl.BlockSpec((None, 6, tn), lambda j, i: (row(i), 0, j))],
        out_specs=pl.BlockSpec((tm, tn), lambda j, i: (i, j)),
        scratch_shapes=[pltpu.VMEM((D, tn), BF16)],
        compiler_params=_cparams(("arbitrary", "arbitrary"), VMEM_LIMIT),
        name="out_proj",
    )(a, w_out, x, mod)


def _router_kernel(h_ref, wt_ref, bias_ref, idx_ref, wgt_ref):
    h1, h2, _ = _split3(h_ref[...])
    w1, w2, _ = _split3(wt_ref[...])
    nt = lambda a, b: lax.dot_general(a, b, (((1,), (1,)), ((), ())), preferred_element_type=F32)
    logits = nt(w1, h1) + nt(w1, h2) + nt(w2, h1)
    scores = _sigmoid(logits)
    sel = scores + bias_ref[...]
    row = lambda a, e: a[e:e + 1, :]
    grp = []
    for g in range(N_GROUPS):
        v = [row(sel, g * EXPERTS_PER_GROUP + j) for j in range(EXPERTS_PER_GROUP)]
        best = None
        for i in range(EXPERTS_PER_GROUP):
            for j in range(i + 1, EXPERTS_PER_GROUP):
                s = v[i] + v[j]
                best = s if best is None else jnp.maximum(best, s)
        grp.append(best)
    gbest = jnp.zeros_like(grp[0], dtype=jnp.int32)
    gval = grp[0]
    for g in range(1, N_GROUPS):
        better = grp[g] > gval
        gbest = jnp.where(better, g, gbest)
        gval = jnp.where(better, grp[g], gval)
    neg = jnp.full_like(gval, -jnp.inf)
    masked = [jnp.where(gbest == (e // EXPERTS_PER_GROUP), row(sel, e), neg) for e in range(N_EXPERTS)]

    def top(excl):
        bi = jnp.full_like(gbest, -1)
        bv = neg
        bs = jnp.zeros_like(gval)
        for e in range(N_EXPERTS):
            cand = masked[e] if excl is None else jnp.where(excl == e, neg, masked[e])
            better = cand > bv
            bi = jnp.where(better, e, bi)
            bv = jnp.where(better, cand, bv)
            bs = jnp.where(better, row(scores, e), bs)
        return bi, bs

    i1, s1 = top(None)
    i2, s2 = top(i1)
    den = s1 + s2
    idx_ref[...] = jnp.concatenate([i1, i2] + [jnp.zeros_like(i1)] * 6, axis=0)
    wgt_ref[...] = jnp.concatenate([s1 / den, s2 / den] + [jnp.zeros_like(s1)] * 6, axis=0)


def _router_call(h2, router_wt, router_bias2):
    N, D = h2.shape
    tm = 512
    return pl.pallas_call(
        _router_kernel,
        out_shape=(jax.ShapeDtypeStruct((8, N), jnp.int32), jax.ShapeDtypeStruct((8, N), F32)),
        grid=(N // tm,),
        in_specs=[pl.BlockSpec((tm, D), lambda i: (i, 0)),
                  pl.BlockSpec((N_EXPERTS, D), lambda i: (0, 0)),
                  pl.BlockSpec((N_EXPERTS, 1), lambda i: (0, 0))],
        out_specs=(pl.BlockSpec((8, tm), lambda i: (0, i)), pl.BlockSpec((8, tm), lambda i: (0, i))),
        compiler_params=_cparams(("arbitrary",), VMEM_LIMIT),
        name="moe_router",
    )(h2, router_wt, router_bias2)


def _row_copy(src_hbm, dst, sem, src_row, dst_row):
    return pltpu.make_async_copy(src_hbm.at[pl.ds(src_row, 1)], dst.at[pl.ds(dst_row, 1)], sem)


def _last_used(i, meta, n_tiles):
    return jnp.minimum(i, meta[n_tiles] - 1)


def _gather_kernel(tok_ref, meta_ref, x_hbm, o_ref, buf, sem, *, tm, n_tiles):
    i = pl.program_id(0)
    n_used = meta_ref[n_tiles]

    def issue(tile, slot):
        base = tile * tm

        def body(g, carry):
            for u in range(DMA_UNROLL):
                r = g * DMA_UNROLL + u
                _row_copy(x_hbm, buf.at[slot], sem.at[slot], tok_ref[base + r], r).start(priority=u % 2)
            return carry

        lax.fori_loop(0, tm // DMA_UNROLL, body, 0)

    @pl.when(i == 0)
    def _():
        issue(0, 0)

    @pl.when(i + 1 < n_used)
    def _():
        issue(i + 1, (i + 1) % 2)

    @pl.when(i < n_used)
    def _():
        slot = i % 2
        pltpu.make_async_copy(x_hbm.at[pl.ds(0, tm)], buf.at[slot], sem.at[slot]).wait()
        o_ref[...] = buf[slot]

    @pl.when(i >= n_used)
    def _():
        o_ref[...] = jnp.zeros_like(o_ref)


def _gather_call(row_tok, meta, x, n_tiles):
    W = x.shape[1]
    tm = MOE_TILE
    return pl.pallas_call(
        functools.partial(_gather_kernel, tm=tm, n_tiles=n_tiles),
        out_shape=jax.ShapeDtypeStruct((n_tiles * tm, W), x.dtype),
        grid_spec=pltpu.PrefetchScalarGridSpec(
            num_scalar_prefetch=2,
            grid=(n_tiles,),
            in_specs=[pl.BlockSpec(memory_space=pl.ANY)],
            out_specs=pl.BlockSpec((tm, W), lambda i, tok, meta: (i, 0)),
            scratch_shapes=[pltpu.VMEM((2, tm, W), x.dtype), pltpu.SemaphoreType.DMA((2,))]),
        compiler_params=_cparams(("arbitrary",), VMEM_LIMIT),
        name="moe_gather",
    )(row_tok, meta, x)


def _expert_changed(meta_ref, i):
    return jnp.logical_or(i == 0, meta_ref[i] != meta_ref[jnp.maximum(i - 1, 0)])


def _expert_up_kernel(meta_ref, x_ref, wg_ref, wu_ref, o_ref, wg_s, wu_s, *, n_tiles):
    i = pl.program_id(1)

    @pl.when(_expert_changed(meta_ref, i))
    def _():
        wg_s[...] = wg_ref[...].astype(BF16)
        wu_s[...] = wu_ref[...].astype(BF16)

    @pl.when(i < meta_ref[n_tiles])
    def _():
        x1, x2 = (t.astype(BF16) for t in _unpack_bf16_pairs(x_ref[...]))
        k = x1.shape[1]
        a = (jnp.dot(x1, wg_s[:k, :], preferred_element_type=F32)
             + jnp.dot(x2, wg_s[k:, :], preferred_element_type=F32))
        b = (jnp.dot(x1, wu_s[:k, :], preferred_element_type=F32)
             + jnp.dot(x2, wu_s[k:, :], preferred_element_type=F32))
        o_ref[...] = (_silu(a) * b).astype(o_ref.dtype)

    @pl.when(i >= meta_ref[n_tiles])
    def _():
        o_ref[...] = jnp.zeros_like(o_ref)


def _expert_up_call(meta, xs, w_gate, w_up, l, n_tiles):
    D = w_gate.shape[-2]
    FF = w_gate.shape[-1]
    tm, tf = MOE_TILE, 512
    w_spec = pl.BlockSpec((None, None, D, tf), lambda f, i, meta: (l, meta[i], 0, f))
    return pl.pallas_call(
        functools.partial(_expert_up_kernel, n_tiles=n_tiles),
        out_shape=jax.ShapeDtypeStruct((n_tiles * tm, FF), BF16),
        grid_spec=pltpu.PrefetchScalarGridSpec(
            num_scalar_prefetch=1,
            grid=(FF // tf, n_tiles),
            in_specs=[pl.BlockSpec((tm, D // 2), lambda f, i, meta: (_last_used(i, meta, n_tiles), 0)),
                      w_spec, w_spec],
            out_specs=pl.BlockSpec((tm, tf), lambda f, i, meta: (i, f)),
            scratch_shapes=[pltpu.VMEM((D, tf), BF16), pltpu.VMEM((D, tf), BF16)]),
        compiler_params=_cparams(("arbitrary", "arbitrary"), VMEM_LIMIT),
        name="moe_expert_up",
    )(meta, xs, w_gate, w_up)


def _expert_down_kernel(meta_ref, h_ref, wd_ref, o_ref, wd_s, *, n_tiles):
    i = pl.program_id(1)

    @pl.when(_expert_changed(meta_ref, i))
    def _():
        wd_s[...] = wd_ref[...].astype(BF16)

    @pl.when(i < meta_ref[n_tiles])
    def _():
        o_ref[...] = _pack_bf16_pairs(jnp.dot(h_ref[...], wd_s[...], preferred_element_type=F32))

    @pl.when(i >= meta_ref[n_tiles])
    def _():
        o_ref[...] = jnp.zeros_like(o_ref)


def _expert_down_call(meta, hmid, w_down, l, n_tiles):
    FF = hmid.shape[1]
    D = w_down.shape[-1]
    tm = MOE_TILE
    return pl.pallas_call(
        functools.partial(_expert_down_kernel, n_tiles=n_tiles),
        out_shape=jax.ShapeDtypeStruct((n_tiles * tm, D // 2), jnp.uint32),
        grid_spec=pltpu.PrefetchScalarGridSpec(
            num_scalar_prefetch=1,
            grid=(1, n_tiles),
            in_specs=[pl.BlockSpec((tm, FF), lambda n, i, meta: (_last_used(i, meta, n_tiles), 0)),
                      pl.BlockSpec((None, None, FF, D), lambda n, i, meta: (l, meta[i], 0, 0))],
            out_specs=pl.BlockSpec((tm, D // 2), lambda n, i, meta: (i, 0)),
            scratch_shapes=[pltpu.VMEM((FF, D), BF16)]),
        compiler_params=_cparams(("arbitrary", "arbitrary"), VMEM_LIMIT),
        name="moe_expert_down",
    )(meta, hmid, w_down)


def _combine_kernel(pos_ref, ys_hbm, x_ref, w_ref, mod_ref, o_ref, buf, sem, *, tm, g_idx, n_tok):
    i = pl.program_id(0)

    def issue(tile, slot):
        base = tile * tm

        def body(g, carry):
            for u in range(DMA_UNROLL):
                r = g * DMA_UNROLL + u
                _row_copy(ys_hbm, buf.at[slot, 0], sem.at[slot, 0], pos_ref[base + r], r).start(priority=0)
                _row_copy(ys_hbm, buf.at[slot, 1], sem.at[slot, 1], pos_ref[n_tok + base + r], r).start(priority=1)
            return carry

        lax.fori_loop(0, tm // DMA_UNROLL, body, 0)

    @pl.when(i == 0)
    def _():
        issue(0, 0)

    @pl.when(i + 1 < n_tok // tm)
    def _():
        issue(i + 1, (i + 1) % 2)

    slot = i % 2
    for k in range(2):
        pltpu.make_async_copy(ys_hbm.at[pl.ds(0, tm)], buf.at[slot, k], sem.at[slot, k]).wait()
    w = w_ref[...]
    a1, a2 = _unpack_bf16_pairs(buf[slot, 0])
    b1, b2 = _unpack_bf16_pairs(buf[slot, 1])
    k = a1.shape[1]
    g = mod_ref[g_idx:g_idx + 1, :]
    o_ref[:, :k] = x_ref[:, :k] + g[:, :k] * (w[:, 0:1] * a1 + w[:, 1:2] * b1)
    o_ref[:, k:] = x_ref[:, k:] + g[:, k:] * (w[:, 0:1] * a2 + w[:, 1:2] * b2)


def _combine_call(pos2, ys, x, w_cols, mod, g_idx, n_prompt, t_s):
    N, D = x.shape
    tm = MOE_TILE
    row = _mod_row_map(tm, n_prompt, t_s)
    return pl.pallas_call(
        functools.partial(_combine_kernel, tm=tm, g_idx=g_idx, n_tok=N),
        out_shape=jax.ShapeDtypeStruct((N, D), F32),
        grid_spec=pltpu.PrefetchScalarGridSpec(
            num_scalar_prefetch=1,
            grid=(N // tm,),
            in_specs=[pl.BlockSpec(memory_space=pl.ANY),
                      pl.BlockSpec((tm, D), lambda i, pos: (i, 0)),
                      pl.BlockSpec((tm, LANES), lambda i, pos: (i, 0)),
                      pl.BlockSpec((None, 6, D), lambda i, pos: (row(i), 0, 0))],
            out_specs=pl.BlockSpec((tm, D), lambda i, pos: (i, 0)),
            scratch_shapes=[pltpu.VMEM((2, 2, tm, D // 2), jnp.uint32), pltpu.SemaphoreType.DMA((2, 2))]),
        compiler_params=_cparams(("arbitrary",), VMEM_LIMIT),
        name="moe_combine",
    )(pos2, ys, x, w_cols, mod)


def _moe_plan(idx, n_tiles):
    N = idx.shape[1]
    tm = MOE_TILE
    blk = 128
    e_flat = idx.reshape(-1)
    experts = jnp.arange(N_EXPERTS, dtype=jnp.int32)
    onehot = (e_flat[:, None] == experts[None, :]).astype(F32)
    oh3 = onehot.reshape(-1, blk, N_EXPERTS)
    tri = (jnp.arange(blk)[:, None] > jnp.arange(blk)[None, :]).astype(F32)
    within = jnp.einsum("ij,bjk->bik", tri, oh3, precision=lax.Precision.HIGHEST)
    blk_tot = jnp.sum(oh3, axis=1)
    before = jnp.cumsum(blk_tot, axis=0) - blk_tot
    rank = jnp.sum((within + before[:, None, :]) * oh3, axis=-1).reshape(-1).astype(jnp.int32)
    counts = jnp.sum(blk_tot, axis=0).astype(jnp.int32)
    tiles = (counts + tm - 1) // tm
    tile_end = jnp.cumsum(tiles)
    tile_start = tile_end - tiles
    start_of = jnp.sum(onehot.astype(jnp.int32) * tile_start[None, :], axis=1)
    pos = (start_of * tm + rank).astype(jnp.int32)
    tok = jnp.tile(jnp.arange(N, dtype=jnp.int32), 2)
    row_tok = jnp.zeros((n_tiles * tm,), jnp.int32).at[pos].set(tok)
    n_used = tile_end[-1]
    t = jnp.arange(n_tiles, dtype=jnp.int32)
    tile_exp = jnp.sum((tile_end[None, :] <= t[:, None]).astype(jnp.int32), axis=1)
    e_last = jnp.max(jnp.where(tiles > 0, experts, 0))
    tile_exp = jnp.where(t < n_used, tile_exp, e_last)
    meta = jnp.concatenate([tile_exp, n_used[None]]).astype(jnp.int32)
    return row_tok, meta, pos


def _moe(h2f, h2p, x, mod, router_wt, router_bias2, w_gate, w_up, w_down, l, n_prompt, t_s):
    N = x.shape[0]
    idx8, wgt8 = _router_call(h2f, router_wt, router_bias2)
    n_tiles = (2 * N) // MOE_TILE + N_EXPERTS
    row_tok, meta, pos2 = _moe_plan(idx8[:2], n_tiles)
    xs = _gather_call(row_tok, meta, h2p, n_tiles)
    hmid = _expert_up_call(meta, xs, w_gate, w_up, l, n_tiles)
    ys = _expert_down_call(meta, hmid, w_down, l, n_tiles)
    w_cols = jnp.zeros((N, LANES), F32).at[:, 0].set(wgt8[0]).at[:, 1].set(wgt8[1])
    return _combine_call(pos2, ys, x, w_cols, mod, 5, n_prompt, t_s)


def _rope_tables(T):
    nf = RET_D // 4
    t = jnp.arange(T)
    pos = jnp.stack([t // GRID_W, t % GRID_W], -1).astype(F32)
    inv = ROPE_BASE ** (-jnp.arange(nf, dtype=F32) / nf)
    ang = pos[:, :, None] * inv
    cos, sin = jnp.cos(ang), jnp.sin(ang)
    cos_t = jnp.concatenate([cos, cos], axis=-1).reshape(T, RET_D)
    sin_t = jnp.concatenate([-sin, sin], axis=-1).reshape(T, RET_D)
    return cos_t, sin_t


def kernel(x_prompt, x_sample, state_ret_fwd, state_ret_bwd, cache_na_k, cache_na_v, state_rwkv_fwd, state_rwkv_bwd, c, c_ctx, w_mod, b_mod, norm_mix, norm_ffn, w_in, w_gate, b_gate, w_br_ret, w_br_na, w_br_rw, w_out, ret_log_decay, ret_gn, na_qn, na_kn, na_rpb, rw_shift, rw_w0, rw_w1, rw_w2, rw_a0, rw_a1, rw_a2, rw_g1, rw_g2, rw_kk, rw_ka, rw_rk, rw_gn, router_w, router_bias, moe_w_gate, moe_w_up, moe_w_down):
    BP, TP, D = x_prompt.shape
    BS, TS, _ = x_sample.shape
    depth = w_in.shape[0]
    NP, NS = BP * TP, BS * TS
    N = NP + NS
    assert NP % (2 * ROW_TILE) == 0 and TS % (2 * ROW_TILE) == 0 and TP % RW_CHUNK == 0 and TS % RW_CHUNK == 0
    assert TS % TP == 0 and NP % TS == 0 and BS <= 7

    x = jnp.concatenate([x_prompt.reshape(NP, D), x_sample.reshape(NS, D)], axis=0)
    c8 = jnp.concatenate([c_ctx[None, :], c, jnp.zeros((7 - BS, D), F32)], axis=0)
    b_mod3 = b_mod.reshape(depth, 1, 6 * D)
    norm_mix3 = norm_mix.reshape(depth, 1, D)
    norm_ffn3 = norm_ffn.reshape(depth, 1, D)
    b_gate3 = b_gate.reshape(depth, 1, 3 * D)
    ret_gn3 = ret_gn.reshape(depth, 1, RET_HEADS * RET_D)
    na_qn3 = na_qn.reshape(depth, 1, NA_DH)
    na_kn3 = na_kn.reshape(depth, 1, NA_DH)
    rw_kk3 = rw_kk.reshape(depth, 1, RW_W)
    rw_ka3 = rw_ka.reshape(depth, 1, RW_W)
    rw_rk3 = rw_rk.reshape(depth, 1, RW_W)
    rw_gn3 = rw_gn.reshape(depth, 1, RW_W)
    zpad = jnp.zeros((depth, D, LANES - 64), F32)
    w_lora = jnp.concatenate([rw_w1[:, 0], zpad, rw_w1[:, 1], zpad, rw_a1[:, 0], zpad, rw_a1[:, 1], zpad,
                              rw_g1], axis=-1)
    kpad = jnp.zeros((depth, 2, LANES - 64, RW_W), F32)
    w2p = jnp.concatenate([rw_w2, kpad], axis=2)
    a2p = jnp.concatenate([rw_a2, kpad], axis=2)
    router_wt = router_w.T
    router_bias2 = router_bias.reshape(N_EXPERTS, 1)
    cos_t, sin_t = _rope_tables(TS)
    zero_st = jnp.zeros((BP, RW_HEADS // 2, LANES, LANES), F32)

    ret_f = jnp.zeros((BP, depth, RET_HEADS, RET_D, RET_D), F32)
    ret_b = jnp.zeros((BP, depth, RET_HEADS, RET_D, RET_D), F32)
    na_k = jnp.zeros((BP, depth, NA_HEADS, TP, NA_DH), F32)
    na_v = jnp.zeros((BP, depth, NA_HEADS, TP, NA_DH), F32)
    rw_f, rw_b = [], []
    for l in range(depth):
        mod = _mod_call(c8, w_mod, b_mod3, l)[:1 + BS].reshape(1 + BS, 6, D)
        (h,) = _norm_mod_call(x, norm_mix3, mod, l, 0, 1, NP, TS, (BF16,))
        proj = _mm_call(h, w_in, l, 2 * ROW_TILE, 512, "in_proj")
        hl = _mm_call(h, w_lora, l, ROW_TILE, LORA_W, "lora_proj")

        y_zero = jnp.zeros((N, RW_W), BF16)
        y_ret, ret_f, ret_b = _ret_call(proj, ret_log_decay[l], ret_gn3, l, BP, TP, 0, False,
                                        (y_zero, ret_f, ret_b))
        y_ret = _ret_call(proj, ret_log_decay[l], ret_gn3, l, BS, TS, NP, True,
                          (cos_t, sin_t, state_ret_fwd, state_ret_bwd, y_ret))
        y_na, na_k, na_v = _na_ctx_call(proj, na_qn3, na_kn3, y_zero, na_k, na_v, l, BP, TP)
        bias = _na_bias(na_rpb[l], TS // GRID_W)
        y_na = _na_lat_call(proj, cache_na_k, cache_na_v, bias, na_qn3, na_kn3, y_na, l, BS, TS, NP)
        y_rw = y_zero
        for (B, T, row0, s0f, s0b) in ((BP, TP, 0, zero_st, zero_st),
                                       (BS, TS, NP, _blockdiag_states(state_rwkv_fwd[:, l]),
                                        _blockdiag_states(state_rwkv_bwd[:, l]))):
            (r, v, al, g, bonus, lw0, be0, kd0, lw1, be1, kd1) = _rw_prep_call(
                proj, hl, rw_shift, w2p, a2p, rw_g2, rw_w0, rw_a0, rw_kk3, rw_ka3, rw_rk3, l, B, T, row0)
            yf, stf = _rw_scan_call(r, v, al, lw0, be0, kd0, s0f, B, T, False)
            yb, stb = _rw_scan_call(r, v, al, lw1, be1, kd1, s0b, B, T, True)
            y_rw = _rw_post_call(yf, yb, bonus, g, rw_gn3, y_rw, l, row0)
            if row0 == 0:
                rw_f.append(_unblock_states(stf))
                rw_b.append(_unblock_states(stb))

        merged = _merge_call(h, y_ret, y_na, y_rw, w_gate, b_gate3, w_br_ret, w_br_na, w_br_rw, l)
        x = _out_call(merged, w_out, x, mod, l, 2, NP, TS)
        h2f, h2p = _norm_mod_call(x, norm_ffn3, mod, l, 3, 4, NP, TS, (F32, jnp.uint32))
        x = _moe(h2f, h2p, x, mod, router_wt, router_bias2, moe_w_gate, moe_w_up, moe_w_down, l, NP, TS)

    y_p = x[:NP].reshape(BP, TP, D)
    y_s = x[NP:].reshape(BS, TS, D)
    return (y_p, y_s, ret_f, ret_b, na_k, na_v, jnp.stack(rw_f, axis=1), jnp.stack(rw_b, axis=1))
```

```python
import functools

import jax
import jax.numpy as jnp
import numpy as np
from jax import lax
from jax.experimental import pallas as pl
from jax.experimental.pallas import tpu as pltpu

F32 = jnp.float32
BF16 = jnp.bfloat16

GRID_W = 64
RET_HEADS = 8
RET_D = 128
NA_HEADS = 8
NA_DH = 128
NA_ROWS = 8
NA_COLS = 16
RW_HEADS = 16
RW_N = 64
RW_W = RW_HEADS * RW_N
N_EXPERTS = 16
N_GROUPS = 4
EXPERTS_PER_GROUP = 4
ROPE_BASE = 10000.0
NORM_EPS = 1e-6
NEG_INF = -1e30

LANES = 128
HEAD_BLOCK = 4
RW_CHUNK = 64
RW_CB = 256
LORA_W = 640
ROW_TILE = 512
MOE_TILE = 256
DMA_UNROLL = 8
VMEM_LIMIT = 56 * 1024 * 1024


def _cparams(sem, vmem=None):
    return pltpu.CompilerParams(dimension_semantics=sem, vmem_limit_bytes=vmem)


def _dot(a, b):
    return jnp.dot(a.astype(BF16), b.astype(BF16), preferred_element_type=F32)


def _dot_nt(a, b):
    return lax.dot_general(a.astype(BF16), b.astype(BF16), (((1,), (1,)), ((), ())),
                           preferred_element_type=F32)


def _dot_tn(a, b):
    return lax.dot_general(a.astype(BF16), b.astype(BF16), (((0,), (0,)), ((), ())),
                           preferred_element_type=F32)


def _split3(x):
    h = x.astype(BF16)
    r1 = x - h.astype(F32)
    m = r1.astype(BF16)
    l = (r1 - m.astype(F32)).astype(BF16)
    return h, m, l


def _dot_exact_lhs(c, x):
    h, m, l = _split3(x)
    cb = c.astype(BF16)
    return (jnp.dot(cb, h, preferred_element_type=F32) + jnp.dot(cb, m, preferred_element_type=F32)
            + jnp.dot(cb, l, preferred_element_type=F32))


def _dot_exact_rhs(x, c):
    h, m, l = _split3(x)
    cb = c.astype(BF16)
    return (jnp.dot(h, cb, preferred_element_type=F32) + jnp.dot(m, cb, preferred_element_type=F32)
            + jnp.dot(l, cb, preferred_element_type=F32))


def _head_sum_matrix():
    r = lax.broadcasted_iota(jnp.int32, (LANES, LANES), 0) // RW_N
    c = lax.broadcasted_iota(jnp.int32, (LANES, LANES), 1) // RW_N
    return (r == c).astype(F32)


def _seg_sum(x, p):
    parts = [_dot_exact_rhs(x[:, s * LANES:(s + 1) * LANES], p) for s in range(x.shape[1] // LANES)]
    return parts[0] if len(parts) == 1 else jnp.concatenate(parts, axis=1)


def _sigmoid(x):
    return 1.0 / (1.0 + jnp.exp(-x))


def _silu(x):
    return x * _sigmoid(x)


def _pack_bf16_pairs(x):
    k = x.shape[1] // 2
    first = lax.bitcast_convert_type(x[:, :k].astype(BF16).astype(F32), jnp.uint32)
    second = lax.bitcast_convert_type(x[:, k:].astype(BF16).astype(F32), jnp.uint32)
    return first | (second >> 16)


def _unpack_bf16_pairs(p):
    first = lax.bitcast_convert_type(p & jnp.uint32(0xFFFF0000), F32)
    second = lax.bitcast_convert_type(p << 16, F32)
    return first, second


def _mod_kernel(c_ref, w_ref, b_ref, o_ref):
    a = _silu(c_ref[...])
    o_ref[...] = _dot(a, w_ref[...]) + b_ref[...]


def _mod_call(c8, w_mod, b_mod3, l):
    D = c8.shape[1]
    n_out = w_mod.shape[2]
    tn = 512
    return pl.pallas_call(
        _mod_kernel,
        out_shape=jax.ShapeDtypeStruct((8, n_out), F32),
        grid=(n_out // tn,),
        in_specs=[pl.BlockSpec((8, D), lambda j: (0, 0)),
                  pl.BlockSpec((None, D, tn), lambda j: (l, 0, j)),
                  pl.BlockSpec((None, 1, tn), lambda j: (l, 0, j))],
        out_specs=pl.BlockSpec((8, tn), lambda j: (0, j)),
        compiler_params=_cparams(("arbitrary",), VMEM_LIMIT),
        name="mod_proj",
    )(c8, w_mod, b_mod3)


def _norm_mod_kernel(x_ref, g_ref, mod_ref, *out_refs, sh_idx, sc_idx):
    x = x_ref[...]
    y = x * lax.rsqrt(jnp.mean(x * x, -1, keepdims=True) + NORM_EPS) * g_ref[...]
    h = y * (1.0 + mod_ref[sc_idx:sc_idx + 1, :]) + mod_ref[sh_idx:sh_idx + 1, :]
    for o in out_refs:
        o[...] = _pack_bf16_pairs(h) if o.dtype == jnp.uint32 else h.astype(o.dtype)


def _mod_row_map(tm, n_prompt, t_s):
    def row(i):
        r = i * tm
        return jnp.where(r < n_prompt, 0, 1 + (r - n_prompt) // t_s)
    return row


def _norm_mod_call(x, g3, mod, l, sh_idx, sc_idx, n_prompt, t_s, out_dtypes):
    N, D = x.shape
    tm = 256
    row = _mod_row_map(tm, n_prompt, t_s)
    width = lambda dt: D // 2 if dt == jnp.uint32 else D
    outs = tuple(jax.ShapeDtypeStruct((N, width(dt)), dt) for dt in out_dtypes)
    return pl.pallas_call(
        functools.partial(_norm_mod_kernel, sh_idx=sh_idx, sc_idx=sc_idx),
        out_shape=outs,
        grid=(N // tm,),
        in_specs=[pl.BlockSpec((tm, D), lambda i: (i, 0)),
                  pl.BlockSpec((None, 1, D), lambda i: (l, 0, 0)),
                  pl.BlockSpec((None, 6, D), lambda i: (row(i), 0, 0))],
        out_specs=tuple(pl.BlockSpec((tm, width(dt)), lambda i: (i, 0)) for dt in out_dtypes),
        compiler_params=_cparams(("arbitrary",), VMEM_LIMIT),
        name="norm_mod",
    )(x, g3, mod)


def _mm_kernel(a_ref, w_ref, o_ref, wbf_ref):
    @pl.when(pl.program_id(1) == 0)
    def _():
        wbf_ref[...] = w_ref[...].astype(BF16)

    o_ref[...] = jnp.dot(a_ref[...], wbf_ref[...], preferred_element_type=F32).astype(o_ref.dtype)


def _mm_call(a, w, l, tm, tn, name, out_dtype=F32):
    N, K = a.shape
    M = w.shape[-1]
    if w.ndim == 3:
        w_spec = pl.BlockSpec((None, K, tn), lambda j, i: (l, 0, j))
    else:
        w_spec = pl.BlockSpec((K, tn), lambda j, i: (0, j))
    return pl.pallas_call(
        _mm_kernel,
        out_shape=jax.ShapeDtypeStruct((N, M), out_dtype),
        grid=(M // tn, N // tm),
        in_specs=[pl.BlockSpec((tm, K), lambda j, i: (i, 0)), w_spec],
        out_specs=pl.BlockSpec((tm, tn), lambda j, i: (i, j)),
        scratch_shapes=[pltpu.VMEM((K, tn), BF16)],
        compiler_params=_cparams(("arbitrary", "arbitrary"), VMEM_LIMIT),
        name=name,
    )(a, w)


def _rope(x, cos, sin):
    lane = lax.broadcasted_iota(jnp.int32, x.shape, 1)
    first = (lane % 64) < 32
    rot = jnp.where(first, pltpu.roll(x, LANES - 32, 1), pltpu.roll(x, 32, 1))
    return x * cos + rot * sin


def _ret_kernel(lg_ref, q_ref, k_ref, v_ref, g_ref, gn_ref, *rest, T, tq, latent):
    if latent:
        cosq_ref, sinq_ref, cosk_ref, sink_ref, s0f_ref, s0b_ref, _, y_ref = rest
    else:
        y_ref, sf_ref, sb_ref = rest[-3:]
    hp = pl.program_id(1)
    qi = pl.program_id(2)
    hs = range(HEAD_BLOCK)
    sl = [slice(j * RET_D, (j + 1) * RET_D) for j in hs]
    lgf = [lg_ref[0, hp * HEAD_BLOCK + j] for j in hs]
    lgb = [lg_ref[1, hp * HEAD_BLOCK + j] for j in hs]
    q = [q_ref[:, s] for s in sl]
    k = [k_ref[:, s] for s in sl]
    v = [v_ref[:, s].astype(BF16) for s in sl]
    if latent:
        q = [_rope(x, cosq_ref[...], sinq_ref[...]) for x in q]
        k = [_rope(x, cosk_ref[...], sink_ref[...]) for x in k]
    k = [x * (RET_D ** -0.5) for x in k]
    t_idx = (lax.broadcasted_iota(jnp.int32, (tq, T), 0) + qi * tq).astype(F32)
    s_idx = lax.broadcasted_iota(jnp.int32, (tq, T), 1).astype(F32)
    diff = t_idx - s_idx
    fwd, bwd = jnp.maximum(diff, 0.0), jnp.maximum(-diff, 0.0)
    dmask = [jnp.where(diff >= 0, jnp.exp(lgf[j] * fwd), 0.0) + jnp.where(diff <= 0, jnp.exp(lgb[j] * bwd), 0.0)
             for j in hs]
    scores = [_dot_nt(q[j], k[j]) * dmask[j] for j in hs]
    y = [_dot(scores[j], v[j]) for j in hs]
    if latent:
        tq_col = (lax.broadcasted_iota(jnp.int32, (tq, 1), 0) + qi * tq).astype(F32)
        y = [y[j] + _dot(q[j] * jnp.exp(lgf[j] * (tq_col + 1.0)), s0f_ref[j])
             + _dot(q[j] * jnp.exp(lgb[j] * (T - tq_col)), s0b_ref[j]) for j in hs]
    else:
        s_col = lax.broadcasted_iota(jnp.int32, (T, 1), 0).astype(F32)
        for j in hs:
            sf_ref[j] = _dot_tn(k[j] * jnp.exp(lgf[j] * (T - 1.0 - s_col)), v[j])
            sb_ref[j] = _dot_tn(k[j] * jnp.exp(lgb[j] * s_col), v[j])
    for j in hs:
        mu = jnp.mean(y[j], -1, keepdims=True)
        yc = y[j] - mu
        var = jnp.mean(yc * yc, -1, keepdims=True)
        yn = yc * lax.rsqrt(var + 1e-5) * gn_ref[:, sl[j]]
        y_ref[:, sl[j]] = (yn * _silu(g_ref[:, sl[j]])).astype(y_ref.dtype)


def _ret_call(proj, lg, gn3, l, B, T, row0, latent, extra):
    N = proj.shape[0]
    tq = 256
    nq = T // tq
    rb = row0 // T
    HB = HEAD_BLOCK
    nh = RET_HEADS // HB
    bw = HB * RET_D
    qmap = lambda b, h, qi: (rb * nq + b * nq + qi, h)
    kmap = lambda b, h, qi: (rb + b, nh + h)
    vmap = lambda b, h, qi: (rb + b, 2 * nh + h)
    gmap = lambda b, h, qi: (rb * nq + b * nq + qi, 3 * nh + h)
    in_specs = [pl.BlockSpec(memory_space=pltpu.SMEM),
                pl.BlockSpec((tq, bw), qmap),
                pl.BlockSpec((T, bw), kmap),
                pl.BlockSpec((T, bw), vmap),
                pl.BlockSpec((tq, bw), gmap),
                pl.BlockSpec((None, 1, bw), lambda b, h, qi: (l, 0, h))]
    args = [lg, proj, proj, proj, proj, gn3]
    y_spec = pl.BlockSpec((tq, bw), qmap)
    y_shape = jax.ShapeDtypeStruct((N, RET_HEADS * RET_D), BF16)
    st_blk = pl.BlockSpec((None, None, HB, RET_D, RET_D), lambda b, h, qi: (b, l, h, 0, 0))
    if latent:
        cos, sin, s0f, s0b, y_buf = extra
        in_specs += [pl.BlockSpec((tq, RET_D), lambda b, h, qi: (qi, 0)),
                     pl.BlockSpec((tq, RET_D), lambda b, h, qi: (qi, 0)),
                     pl.BlockSpec((T, RET_D), lambda b, h, qi: (0, 0)),
                     pl.BlockSpec((T, RET_D), lambda b, h, qi: (0, 0)),
                     st_blk, st_blk, pl.BlockSpec(memory_space=pl.ANY)]
        args += [cos, sin, cos, sin, s0f, s0b, y_buf]
        aliases = {len(args) - 1: 0}
        out_shape = y_shape
        out_specs = y_spec
    else:
        y_buf, sf_buf, sb_buf = extra
        st = jax.ShapeDtypeStruct(sf_buf.shape, F32)
        st_spec = st_blk
        in_specs += [pl.BlockSpec(memory_space=pl.ANY)] * 3
        args += [y_buf, sf_buf, sb_buf]
        aliases = {len(args) - 3: 0, len(args) - 2: 1, len(args) - 1: 2}
        out_shape = (y_shape, st, st)
        out_specs = (y_spec, st_spec, st_spec)
    return pl.pallas_call(
        functools.partial(_ret_kernel, T=T, tq=tq, latent=latent),
        out_shape=out_shape,
        grid=(B, nh, nq),
        in_specs=in_specs,
        out_specs=out_specs,
        input_output_aliases=aliases,
        compiler_params=_cparams(("arbitrary", "arbitrary", "arbitrary"), VMEM_LIMIT),
        name="retention_latent" if latent else "retention_ctx",
    )(*args)


def _rms(x, g):
    return x * lax.rsqrt(jnp.mean(x * x, -1, keepdims=True) + NORM_EPS) * g


def _na_ctx_kernel(q_ref, k_ref, v_ref, qn_ref, kn_ref, *rest):
    y_ref, ko_ref, vo_ref = rest[-3:]
    hs = range(HEAD_BLOCK)
    sl = [slice(j * NA_DH, (j + 1) * NA_DH) for j in hs]
    q = [_rms(q_ref[:, s], qn_ref[...]) * (NA_DH ** -0.5) for s in sl]
    k = [_rms(k_ref[:, s], kn_ref[...]) for s in sl]
    v = [v_ref[:, s] for s in sl]
    s = [_dot_nt(q[j], k[j]) for j in hs]
    p = [jnp.exp(x - jnp.max(x, -1, keepdims=True)) for x in s]
    o = [_dot(p[j], v[j]) for j in hs]
    for j in hs:
        y_ref[:, sl[j]] = (o[j] / jnp.sum(p[j], -1, keepdims=True)).astype(y_ref.dtype)
        ko_ref[j] = k[j]
        vo_ref[j] = v[j]


def _na_ctx_call(proj, qn3, kn3, y_buf, k_buf, v_buf, l, B, T):
    N = proj.shape[0]
    HB = HEAD_BLOCK
    nh = NA_HEADS // HB
    bw = HB * NA_DH
    c0 = 4 * (RET_HEADS // HB)
    kv = jax.ShapeDtypeStruct(k_buf.shape, F32)
    kv_spec = pl.BlockSpec((None, None, HB, T, NA_DH), lambda b, h: (b, l, h, 0, 0))
    any_spec = pl.BlockSpec(memory_space=pl.ANY)
    return pl.pallas_call(
        _na_ctx_kernel,
        out_shape=(jax.ShapeDtypeStruct((N, NA_HEADS * NA_DH), BF16), kv, kv),
        grid=(B, nh),
        in_specs=[pl.BlockSpec((T, bw), lambda b, h: (b, c0 + h)),
                  pl.BlockSpec((T, bw), lambda b, h: (b, c0 + nh + h)),
                  pl.BlockSpec((T, bw), lambda b, h: (b, c0 + 2 * nh + h)),
                  pl.BlockSpec((None, 1, NA_DH), lambda b, h: (l, 0, 0)),
                  pl.BlockSpec((None, 1, NA_DH), lambda b, h: (l, 0, 0)),
                  any_spec, any_spec, any_spec],
        out_specs=(pl.BlockSpec((T, bw), lambda b, h: (b, h)), kv_spec, kv_spec),
        input_output_aliases={5: 0, 6: 1, 7: 2},
        compiler_params=_cparams(("arbitrary", "arbitrary"), VMEM_LIMIT),
        name="na_ctx",
    )(proj, proj, proj, qn3, kn3, y_buf, k_buf, v_buf)


def _na_lat_kernel(q_ref, k_ref, v_ref, ck_ref, cv_ref, bias_ref, qn_ref, kn_ref, _, y_ref, *, rows, kr):
    q = _rms(q_ref[...], qn_ref[...]) * (NA_DH ** -0.5)
    k = _rms(k_ref[...], kn_ref[...]).astype(BF16)
    v = v_ref[...].astype(BF16)
    ck = ck_ref[...].astype(BF16)
    cv = cv_ref[...].astype(BF16)
    nl = kr * GRID_W
    qb = q.astype(BF16)
    starts = [min(max(r - kr // 2, 0), rows - kr) for r in range(rows)]
    groups = []
    for r, rs in enumerate(starts):
        if groups and groups[-1][0] == rs:
            groups[-1][2] = r + 1
        else:
            groups.append([rs, r, r + 1])
    W = GRID_W
    s_ctx = _dot_nt(qb, ck)
    s_loc = [_dot_nt(qb[r0 * W:r1 * W], k[rs * W:rs * W + nl])
             + bias_ref[r0:r1].reshape((r1 - r0) * W, nl) for rs, r0, r1 in groups]
    p_loc, p_ctx, den = [], [], []
    for (rs, r0, r1), sl in zip(groups, s_loc):
        sc = s_ctx[r0 * W:r1 * W]
        m = jnp.maximum(jnp.max(sl, -1, keepdims=True), jnp.max(sc, -1, keepdims=True))
        pl_, pc_ = jnp.exp(sl - m), jnp.exp(sc - m)
        p_loc.append(pl_.astype(BF16))
        p_ctx.append(pc_.astype(BF16))
        den.append(jnp.sum(pl_, -1, keepdims=True) + jnp.sum(pc_, -1, keepdims=True))
    o_ctx = _dot(jnp.concatenate(p_ctx, axis=0), cv)
    o_loc = [_dot(p, v[rs * W:rs * W + nl]) for (rs, r0, r1), p in zip(groups, p_loc)]
    for (rs, r0, r1), ol, d in zip(groups, o_loc, den):
        y_ref[r0 * W:r1 * W, :] = ((ol + o_ctx[r0 * W:r1 * W]) / d).astype(y_ref.dtype)


def _na_bias(rpb, rows):
    H = rpb.shape[0]
    kr = min(NA_ROWS, rows)
    r_ids = np.arange(rows)
    row_start = np.clip(r_ids - kr // 2, 0, rows - kr)
    row_idx = row_start[:, None] + np.arange(kr)[None, :]
    cols = np.arange(GRID_W)
    col_start = np.clip(cols - NA_COLS // 2, 0, GRID_W - NA_COLS)
    col_ok = (cols[None, :] >= col_start[:, None]) & (cols[None, :] < col_start[:, None] + NA_COLS)
    dr_idx = row_idx - r_ids[:, None] + (NA_ROWS - 1)
    dc_idx = np.clip(cols[None, :] - cols[:, None] + (NA_COLS - 1), 0, 2 * NA_COLS - 2)
    rpb = rpb.astype(F32)
    blk = jnp.zeros((H, 2 * NA_ROWS - 1, GRID_W, GRID_W), F32)
    for d in range(2 * NA_COLS - 1):
        blk = jnp.where(jnp.asarray(dc_idx == d)[None, None], rpb[:, :, d][:, :, None, None], blk)
    blk = jnp.where(jnp.asarray(col_ok)[None, None], blk, NEG_INF)
    rows_out = [jnp.concatenate([blk[:, int(dr_idx[r, j])] for j in range(kr)], axis=-1) for r in range(rows)]
    return jnp.stack(rows_out, axis=1)


def _na_lat_call(proj, cache_k, cache_v, bias, qn3, kn3, y_prev, l, B, T, row0):
    N = proj.shape[0]
    c0 = 4 * RET_HEADS
    rb = row0 // T
    rows = T // GRID_W
    kr = min(NA_ROWS, rows)
    L = cache_k.shape[3]
    return pl.pallas_call(
        functools.partial(_na_lat_kernel, rows=rows, kr=kr),
        out_shape=jax.ShapeDtypeStruct((N, NA_HEADS * NA_DH), BF16),
        grid=(B, NA_HEADS),
        in_specs=[pl.BlockSpec((T, NA_DH), lambda b, h: (rb + b, c0 + h)),
                  pl.BlockSpec((T, NA_DH), lambda b, h: (rb + b, c0 + NA_HEADS + h)),
                  pl.BlockSpec((T, NA_DH), lambda b, h: (rb + b, c0 + 2 * NA_HEADS + h)),
                  pl.BlockSpec((None, None, None, L, NA_DH), lambda b, h: (b, l, h, 0, 0)),
                  pl.BlockSpec((None, None, None, L, NA_DH), lambda b, h: (b, l, h, 0, 0)),
                  pl.BlockSpec((None, rows, GRID_W, kr * GRID_W), lambda b, h: (h, 0, 0, 0)),
                  pl.BlockSpec((None, 1, NA_DH), lambda b, h: (l, 0, 0)),
                  pl.BlockSpec((None, 1, NA_DH), lambda b, h: (l, 0, 0)),
                  pl.BlockSpec(memory_space=pl.ANY)],
        out_specs=pl.BlockSpec((T, NA_DH), lambda b, h: (rb + b, h)),
        input_output_aliases={8: 0},
        compiler_params=_cparams(("arbitrary", "arbitrary"), VMEM_LIMIT),
        name="na_latent",
    )(proj, proj, proj, cache_k, cache_v, bias, qn3, kn3, y_prev)


def _rw_prep_kernel(xr_ref, xk_ref, xv_ref, shr_ref, shk_ref, shv_ref, hl_ref, w2_ref, a2_ref, g2_ref,
                    w0_ref, a0_ref, kk_ref, ka_ref, rk_ref,
                    r_o, v_o, al_o, g_o, bon_o, lw0_o, be0_o, kd0_o, lw1_o, be1_o, kd1_o, *, T):
    row = lax.broadcasted_iota(jnp.int32, (T, 1), 0)

    def shift(x_ref, sh_ref):
        x = x_ref[...]
        prev = jnp.where(row == 0, 0.0, pltpu.roll(x, 1, 0))
        nxt = jnp.where(row == T - 1, 0.0, pltpu.roll(x, T - 1, 0))
        return sh_ref[0:1, :] * prev + sh_ref[1:2, :] * x + sh_ref[2:3, :] * nxt

    r = shift(xr_ref, shr_ref)
    k = shift(xk_ref, shk_ref)
    v = shift(xv_ref, shv_ref)
    p = _head_sum_matrix()
    hl = hl_ref[...]
    g = _dot(_sigmoid(hl[:, 4 * LANES:5 * LANES]), g2_ref[...])
    kk = k * kk_ref[...]
    kk = kk / jnp.maximum(jnp.sqrt(_seg_sum(kk * kk, p)), 1e-6)
    bonus = _seg_sum(r * k * rk_ref[...], p) * v
    r_o[...] = r
    v_o[...] = v
    al_o[...] = -kk
    g_o[...] = g
    bon_o[...] = bonus
    outs = ((lw0_o, be0_o, kd0_o), (lw1_o, be1_o, kd1_o))
    for d in range(2):
        z = w0_ref[d:d + 1, :] + _dot(jnp.tanh(hl[:, d * LANES:(d + 1) * LANES]), w2_ref[d])
        u = -z
        softplus = jnp.maximum(u, 0.0) + jnp.log(1.0 + jnp.exp(-jnp.abs(u)))
        wlog = -softplus - 0.5
        a = _sigmoid(a0_ref[d:d + 1, :] + _dot(hl[:, (2 + d) * LANES:(3 + d) * LANES], a2_ref[d]))
        lw_o, be_o, kd_o = outs[d]
        lw_o[...] = -jnp.exp(wlog)
        be_o[...] = kk * a
        kd_o[...] = k * (1.0 + (a - 1.0) * ka_ref[...])


def _rw_prep_call(proj, hl, shift3, w2p, a2p, g2, w0, a0, kk3, ka3, rk3, l, B, T, row0):
    rb = row0 // T
    nc = RW_W // RW_CB
    c0 = 7 * (RW_W // RW_CB)
    n_rows = B * T
    big = lambda off: pl.BlockSpec((T, RW_CB), lambda b, c: (rb + b, c0 + off * nc + c))
    sh = lambda off: pl.BlockSpec((None, 3, RW_CB), lambda b, c: (l, 0, off * nc + c))
    vec = pl.BlockSpec((None, 1, RW_CB), lambda b, c: (l, 0, c))
    out_spec = pl.BlockSpec((T, RW_CB), lambda b, c: (b, c))
    out = jax.ShapeDtypeStruct((n_rows, RW_W), F32)
    return pl.pallas_call(
        functools.partial(_rw_prep_kernel, T=T),
        out_shape=(out,) * 11,
        grid=(B, nc),
        in_specs=[big(0), big(1), big(2), sh(0), sh(1), sh(2),
                  pl.BlockSpec((T, LORA_W), lambda b, c: (rb + b, 0)),
                  pl.BlockSpec((None, 2, LANES, RW_CB), lambda b, c: (l, 0, 0, c)),
                  pl.BlockSpec((None, 2, LANES, RW_CB), lambda b, c: (l, 0, 0, c)),
                  pl.BlockSpec((None, LANES, RW_CB), lambda b, c: (l, 0, c)),
                  pl.BlockSpec((None, 2, RW_CB), lambda b, c: (l, 0, c)),
                  pl.BlockSpec((None, 2, RW_CB), lambda b, c: (l, 0, c)),
                  vec, vec, vec],
        out_specs=(out_spec,) * 11,
        compiler_params=_cparams(("arbitrary", "arbitrary"), VMEM_LIMIT),
        name="rwkv_prep",
    )(proj, proj, proj, shift3, shift3, shift3, hl, w2p, a2p, g2, w0, a0, kk3, ka3, rk3)


def _rw_scan_kernel(r_ref, v_ref, al_ref, lw_ref, be_ref, kd_ref, s0_ref, y_ref, sf_ref, st_ref,
                    *, reverse, nchunks):
    C = RW_CHUNK
    c = pl.program_id(1)

    @pl.when(c == 0)
    def _():
        st_ref[...] = s0_ref[...]

    ti = lax.broadcasted_iota(jnp.int32, (C, C), 0)
    si = lax.broadcasted_iota(jnp.int32, (C, C), 1)
    incl = (ti <= si) if reverse else (ti >= si)
    lw = lw_ref[...]
    cum = _dot_exact_lhs(incl.astype(F32), lw)
    tot = jnp.sum(lw, axis=0, keepdims=True)
    e_incl = jnp.exp(cum)
    e_excl = jnp.exp(cum - lw)
    e_neg = jnp.exp(-cum)
    e_rem = jnp.exp(tot - cum)
    g_tot = jnp.exp(tot)
    be = be_ref[...]
    kd = kd_ref[...]
    scaled = (al_ref[...] * e_excl, r_ref[...] * e_incl, be * e_neg, kd * e_neg, v_ref[...],
              be * e_rem, kd * e_rem)

    S = 2 * C
    t2 = lax.broadcasted_iota(jnp.int32, (S, S), 0)
    s2 = lax.broadcasted_iota(jnp.int32, (S, S), 1)
    same = (t2 // C) == (s2 // C)
    if reverse:
        strict2 = same & (t2 < s2)
        incl2 = same & (t2 <= s2)
    else:
        strict2 = same & (t2 > s2)
        incl2 = same & (t2 >= s2)
    eye = t2 == s2
    lane = lax.broadcasted_iota(jnp.int32, (C, LANES), 1)
    m0 = (lane < RW_N).astype(F32)
    m1 = 1.0 - m0

    def stack(x):
        return jnp.concatenate([x * m0, x * m1], axis=0).astype(BF16)

    pairs = range(RW_HEADS // 2)
    cat = jnp.concatenate
    sls = [slice(p * LANES, (p + 1) * LANES) for p in pairs]
    a_s, r_s, b_s, k_s, v_s, bg_s, kg_s = ([stack(x[:, sl]) for sl in sls] for x in scaled)
    hst = [st_ref[p] for p in pairs]
    h_b = [h.astype(BF16) for h in hst]
    m_all = [_dot_nt(cat([a_s[p], r_s[p]], 0), cat([b_s[p], k_s[p]], 0)) for p in pairs]
    m_ab = [jnp.where(strict2, m_all[p][:S, :S], 0.0) for p in pairs]
    m_ak = [jnp.where(strict2, m_all[p][:S, S:], 0.0).astype(BF16) for p in pairs]
    m_rb = [jnp.where(incl2, m_all[p][S:, :S], 0.0).astype(BF16) for p in pairs]
    m_rk = [jnp.where(incl2, m_all[p][S:, S:], 0.0).astype(BF16) for p in pairs]
    x = [_dot(cat([a_s[p], m_ak[p]], 1), cat([h_b[p], v_s[p]], 0)) for p in pairs]
    pw = m_ab
    n_steps = int(np.log2(C))
    for step in range(n_steps):
        if step + 1 < n_steps:
            px = [_dot(pw[p], cat([x[p], pw[p]], 1)) for p in pairs]
            x = [x[p] + px[p][:, :S] for p in pairs]
            pw = [px[p][:, S:] for p in pairs]
        else:
            x = [x[p] + _dot(pw[p], x[p]) for p in pairs]
    u_s = [x[p].astype(BF16) for p in pairs]
    y_s = [_dot(cat([r_s[p], m_rb[p], m_rk[p]], 1), cat([h_b[p], u_s[p], v_s[p]], 0)) for p in pairs]
    upd = [_dot_tn(cat([bg_s[p], kg_s[p]], 0), cat([u_s[p], v_s[p]], 0)) for p in pairs]
    for p in pairs:
        y_ref[:, sls[p]] = y_s[p][:C] + y_s[p][C:]
        g_col = jnp.sum(jnp.where(eye, jnp.broadcast_to(g_tot[:, sls[p]], (S, S)), 0.0), axis=1, keepdims=True)
        st_ref[p] = hst[p] * g_col + upd[p]

    @pl.when(c == nchunks - 1)
    def _():
        sf_ref[...] = st_ref[...]


def _rw_scan_call(r, v, al, lw, be, kd, s0, B, T, reverse):
    C = RW_CHUNK
    nch = T // C
    if reverse:
        cmap = lambda b, c: (b * nch + (nch - 1 - c), 0)
    else:
        cmap = lambda b, c: (b * nch + c, 0)
    blk = pl.BlockSpec((C, RW_W), cmap)
    st_spec = pl.BlockSpec((None, RW_HEADS // 2, LANES, LANES), lambda b, c: (b, 0, 0, 0))
    return pl.pallas_call(
        functools.partial(_rw_scan_kernel, reverse=reverse, nchunks=nch),
        out_shape=(jax.ShapeDtypeStruct((B * T, RW_W), F32),
                   jax.ShapeDtypeStruct((B, RW_HEADS // 2, LANES, LANES), F32)),
        grid=(B, nch),
        in_specs=[blk] * 6 + [st_spec],
        out_specs=(blk, st_spec),
        scratch_shapes=[pltpu.VMEM((RW_HEADS // 2, LANES, LANES), F32)],
        compiler_params=_cparams(("arbitrary", "arbitrary"), VMEM_LIMIT),
        name="rwkv_scan_bwd" if reverse else "rwkv_scan_fwd",
    )(r, v, al, lw, be, kd, s0)


def _rw_post_kernel(yf_ref, yb_ref, bon_ref, g_ref, gn_ref, *rest):
    o_ref = rest[-1]
    p = _head_sum_matrix()
    y = yf_ref[...] + yb_ref[...]
    mu = _seg_sum(y, p) * (1.0 / RW_N)
    yc = y - mu
    var = _seg_sum(yc * yc, p) * (1.0 / RW_N)
    yn = yc * lax.rsqrt(var + 64e-5) * gn_ref[...]
    o_ref[...] = ((yn + bon_ref[...]) * g_ref[...]).astype(o_ref.dtype)


def _rw_post_call(yf, yb, bonus, g, gn3, y_buf, l, row0):
    n_rows = yf.shape[0]
    tm = 256
    blk = pl.BlockSpec((tm, RW_W), lambda i: (i, 0))
    return pl.pallas_call(
        _rw_post_kernel,
        out_shape=jax.ShapeDtypeStruct(y_buf.shape, BF16),
        grid=(n_rows // tm,),
        in_specs=[blk, blk, blk, blk, pl.BlockSpec((None, 1, RW_W), lambda i: (l, 0, 0)),
                  pl.BlockSpec(memory_space=pl.ANY)],
        out_specs=pl.BlockSpec((tm, RW_W), lambda i: (row0 // tm + i, 0)),
        input_output_aliases={5: 0},
        compiler_params=_cparams(("arbitrary",), VMEM_LIMIT),
        name="rwkv_post",
    )(yf, yb, bonus, g, gn3, y_buf)


def _blockdiag_states(s):
    B = s.shape[0]
    s = jnp.swapaxes(s, -1, -2).reshape(B, RW_HEADS // 2, 2, RW_N, RW_N)
    z = jnp.zeros_like(s[:, :, 0])
    top = jnp.concatenate([s[:, :, 0], z], axis=-1)
    bot = jnp.concatenate([z, s[:, :, 1]], axis=-1)
    return jnp.concatenate([top, bot], axis=-2)


def _unblock_states(sd):
    B = sd.shape[0]
    h0 = sd[:, :, :RW_N, :RW_N]
    h1 = sd[:, :, RW_N:, RW_N:]
    return jnp.swapaxes(jnp.stack([h0, h1], axis=2).reshape(B, RW_HEADS, RW_N, RW_N), -1, -2)


def _merge_kernel(h_ref, ya_ref, yb_ref, yc_ref, wga_ref, wgb_ref, wgc_ref, bga_ref, bgb_ref, bgc_ref,
                  wa_ref, wb_ref, wc_ref, o_ref, wg_s, wbr_s):
    @pl.when(pl.program_id(1) == 0)
    def _():
        for n, (wg, wb) in enumerate(((wga_ref, wa_ref), (wgb_ref, wb_ref), (wgc_ref, wc_ref))):
            wg_s[n] = wg[...].astype(BF16)
            wbr_s[n] = wb[...].astype(BF16)

    h = h_ref[...]
    acc = None
    for n, (y_ref, bg_ref) in enumerate(((ya_ref, bga_ref), (yb_ref, bgb_ref), (yc_ref, bgc_ref))):
        gate = _sigmoid(jnp.dot(h, wg_s[n], preferred_element_type=F32) + bg_ref[...])
        term = gate * jnp.dot(y_ref[...], wbr_s[n], preferred_element_type=F32)
        acc = term if acc is None else acc + term
    o_ref[...] = acc.astype(o_ref.dtype)


def _merge_call(h, y_ret, y_na, y_rw, w_gate, b_gate3, w_ret, w_na, w_rw, l):
    N, D = h.shape
    KB = y_ret.shape[1]
    tm, tn = ROW_TILE, 256
    nj = D // tn
    a_spec = pl.BlockSpec((tm, D), lambda j, i: (i, 0))
    y_spec = pl.BlockSpec((tm, KB), lambda j, i: (i, 0))
    once = pl.Buffered(1)
    wg = lambda n: pl.BlockSpec((None, D, tn), lambda j, i: (l, 0, n * nj + j))
    bg = lambda n: pl.BlockSpec((None, 1, tn), lambda j, i: (l, 0, n * nj + j))
    wb = pl.BlockSpec((None, KB, tn), lambda j, i: (l, 0, j), pipeline_mode=once)
    return pl.pallas_call(
        _merge_kernel,
        out_shape=jax.ShapeDtypeStruct((N, D), BF16),
        grid=(nj, N // tm),
        in_specs=[a_spec, y_spec, y_spec, y_spec, wg(0), wg(1), wg(2), bg(0), bg(1), bg(2), wb, wb, wb],
        out_specs=pl.BlockSpec((tm, tn), lambda j, i: (i, j)),
        scratch_shapes=[pltpu.VMEM((3, D, tn), BF16), pltpu.VMEM((3, KB, tn), BF16)],
        compiler_params=_cparams(("arbitrary", "arbitrary"), VMEM_LIMIT),
        name="gated_merge",
    )(h, y_ret, y_na, y_rw, w_gate, w_gate, w_gate, b_gate3, b_gate3, b_gate3, w_ret, w_na, w_rw)


def _out_kernel(a_ref, w_ref, x_ref, mod_ref, o_ref, wbf_ref, *, g_idx):
    @pl.when(pl.program_id(1) == 0)
    def _():
        wbf_ref[...] = w_ref[...].astype(BF16)

    m = jnp.dot(a_ref[...], wbf_ref[...], preferred_element_type=F32)
    o_ref[...] = x_ref[...] + mod_ref[g_idx:g_idx + 1, :] * m


def _out_call(a, w_out, x, mod, l, g_idx, n_prompt, t_s):
    N, D = x.shape
    tm, tn = 2 * ROW_TILE, 512
    row = _mod_row_map(tm, n_prompt, t_s)
    return pl.pallas_call(
        functools.partial(_out_kernel, g_idx=g_idx),
        out_shape=jax.ShapeDtypeStruct((N, D), F32),
        grid=(D // tn, N // tm),
        in_specs=[pl.BlockSpec((tm, D), lambda j, i: (i, 0)),
                  pl.BlockSpec((None, D, tn), lambda j, i: (l, 0, j)),
                  pl.BlockSpec((tm, tn), lambda j, i: (i, j)),
                  pl.BlockSpec((None, 6, tn), lambda j, i: (row(i), 0, j))],
        out_specs=pl.BlockSpec((tm, tn), lambda j, i: (i, j)),
        scratch_shapes=[pltpu.VMEM((D, tn), BF16)],
        compiler_params=_cparams(("arbitrary", "arbitrary"), VMEM_LIMIT),
        name="out_proj",
    )(a, w_out, x, mod)


def _router_kernel(h_ref, wt_ref, bias_ref, idx_ref, wgt_ref):
    h1, h2, _ = _split3(h_ref[...])
    w1, w2, _ = _split3(wt_ref[...])
    nt = lambda a, b: lax.dot_general(a, b, (((1,), (1,)), ((), ())), preferred_element_type=F32)
    logits = nt(w1, h1) + nt(w1, h2) + nt(w2, h1)
    scores = _sigmoid(logits)
    sel = scores + bias_ref[...]
    row = lambda a, e: a[e:e + 1, :]
    grp = []
    for g in range(N_GROUPS):
        v = [row(sel, g * EXPERTS_PER_GROUP + j) for j in range(EXPERTS_PER_GROUP)]
        best = None
        for i in range(EXPERTS_PER_GROUP):
            for j in range(i + 1, EXPERTS_PER_GROUP):
                s = v[i] + v[j]
                best = s if best is None else jnp.maximum(best, s)
        grp.append(best)
    gbest = jnp.zeros_like(grp[0], dtype=jnp.int32)
    gval = grp[0]
    for g in range(1, N_GROUPS):
        better = grp[g] > gval
        gbest = jnp.where(better, g, gbest)
        gval = jnp.where(better, grp[g], gval)
    neg = jnp.full_like(gval, -jnp.inf)
    masked = [jnp.where(gbest == (e // EXPERTS_PER_GROUP), row(sel, e), neg) for e in range(N_EXPERTS)]

    def top(excl):
        bi = jnp.full_like(gbest, -1)
        bv = neg
        bs = jnp.zeros_like(gval)
        for e in range(N_EXPERTS):
            cand = masked[e] if excl is None else jnp.where(excl == e, neg, masked[e])
            better = cand > bv
            bi = jnp.where(better, e, bi)
            bv = jnp.where(better, cand, bv)
            bs = jnp.where(better, row(scores, e), bs)
        return bi, bs

    i1, s1 = top(None)
    i2, s2 = top(i1)
    den = s1 + s2
    idx_ref[...] = jnp.concatenate([i1, i2] + [jnp.zeros_like(i1)] * 6, axis=0)
    wgt_ref[...] = jnp.concatenate([s1 / den, s2 / den] + [jnp.zeros_like(s1)] * 6, axis=0)


def _router_call(h2, router_wt, router_bias2):
    N, D = h2.shape
    tm = 512
    return pl.pallas_call(
        _router_kernel,
        out_shape=(jax.ShapeDtypeStruct((8, N), jnp.int32), jax.ShapeDtypeStruct((8, N), F32)),
        grid=(N // tm,),
        in_specs=[pl.BlockSpec((tm, D), lambda i: (i, 0)),
                  pl.BlockSpec((N_EXPERTS, D), lambda i: (0, 0)),
                  pl.BlockSpec((N_EXPERTS, 1), lambda i: (0, 0))],
        out_specs=(pl.BlockSpec((8, tm), lambda i: (0, i)), pl.BlockSpec((8, tm), lambda i: (0, i))),
        compiler_params=_cparams(("arbitrary",), VMEM_LIMIT),
        name="moe_router",
    )(h2, router_wt, router_bias2)


def _row_copy(src_hbm, dst, sem, src_row, dst_row):
    return pltpu.make_async_copy(src_hbm.at[pl.ds(src_row, 1)], dst.at[pl.ds(dst_row, 1)], sem)


def _last_used(i, meta, n_tiles):
    return jnp.minimum(i, meta[n_tiles] - 1)


def _gather_kernel(tok_ref, meta_ref, x_hbm, o_ref, buf, sem, *, tm, n_tiles):
    i = pl.program_id(0)
    n_used = meta_ref[n_tiles]

    def issue(tile, slot):
        base = tile * tm

        def body(g, carry):
            for u in range(DMA_UNROLL):
                r = g * DMA_UNROLL + u
                _row_copy(x_hbm, buf.at[slot], sem.at[slot], tok_ref[base + r], r).start(priority=u % 2)
            return carry

        lax.fori_loop(0, tm // DMA_UNROLL, body, 0)

    @pl.when(i == 0)
    def _():
        issue(0, 0)

    @pl.when(i + 1 < n_used)
    def _():
        issue(i + 1, (i + 1) % 2)

    @pl.when(i < n_used)
    def _():
        slot = i % 2
        pltpu.make_async_copy(x_hbm.at[pl.ds(0, tm)], buf.at[slot], sem.at[slot]).wait()
        o_ref[...] = buf[slot]

    @pl.when(i >= n_used)
    def _():
        o_ref[...] = jnp.zeros_like(o_ref)


def _gather_call(row_tok, meta, x, n_tiles):
    W = x.shape[1]
    tm = MOE_TILE
    return pl.pallas_call(
        functools.partial(_gather_kernel, tm=tm, n_tiles=n_tiles),
        out_shape=jax.ShapeDtypeStruct((n_tiles * tm, W), x.dtype),
        grid_spec=pltpu.PrefetchScalarGridSpec(
            num_scalar_prefetch=2,
            grid=(n_tiles,),
            in_specs=[pl.BlockSpec(memory_space=pl.ANY)],
            out_specs=pl.BlockSpec((tm, W), lambda i, tok, meta: (i, 0)),
            scratch_shapes=[pltpu.VMEM((2, tm, W), x.dtype), pltpu.SemaphoreType.DMA((2,))]),
        compiler_params=_cparams(("arbitrary",), VMEM_LIMIT),
        name="moe_gather",
    )(row_tok, meta, x)


def _expert_changed(meta_ref, i):
    return jnp.logical_or(i == 0, meta_ref[i] != meta_ref[jnp.maximum(i - 1, 0)])


def _expert_up_kernel(meta_ref, x_ref, wg_ref, wu_ref, o_ref, wg_s, wu_s, *, n_tiles):
    i = pl.program_id(1)

    @pl.when(_expert_changed(meta_ref, i))
    def _():
        wg_s[...] = wg_ref[...].astype(BF16)
        wu_s[...] = wu_ref[...].astype(BF16)

    @pl.when(i < meta_ref[n_tiles])
    def _():
        x1, x2 = (t.astype(BF16) for t in _unpack_bf16_pairs(x_ref[...]))
        k = x1.shape[1]
        a = (jnp.dot(x1, wg_s[:k, :], preferred_element_type=F32)
             + jnp.dot(x2, wg_s[k:, :], preferred_element_type=F32))
        b = (jnp.dot(x1, wu_s[:k, :], preferred_element_type=F32)
             + jnp.dot(x2, wu_s[k:, :], preferred_element_type=F32))
        o_ref[...] = (_silu(a) * b).astype(o_ref.dtype)

    @pl.when(i >= meta_ref[n_tiles])
    def _():
        o_ref[...] = jnp.zeros_like(o_ref)


def _expert_up_call(meta, xs, w_gate, w_up, l, n_tiles):
    D = w_gate.shape[-2]
    FF = w_gate.shape[-1]
    tm, tf = MOE_TILE, 512
    w_spec = pl.BlockSpec((None, None, D, tf), lambda f, i, meta: (l, meta[i], 0, f))
    return pl.pallas_call(
        functools.partial(_expert_up_kernel, n_tiles=n_tiles),
        out_shape=jax.ShapeDtypeStruct((n_tiles * tm, FF), BF16),
        grid_spec=pltpu.PrefetchScalarGridSpec(
            num_scalar_prefetch=1,
            grid=(FF // tf, n_tiles),
            in_specs=[pl.BlockSpec((tm, D // 2), lambda f, i, meta: (_last_used(i, meta, n_tiles), 0)),
                      w_spec, w_spec],
            out_specs=pl.BlockSpec((tm, tf), lambda f, i, meta: (i, f)),
            scratch_shapes=[pltpu.VMEM((D, tf), BF16), pltpu.VMEM((D, tf), BF16)]),
        compiler_params=_cparams(("arbitrary", "arbitrary"), VMEM_LIMIT),
        name="moe_expert_up",
    )(meta, xs, w_gate, w_up)


def _expert_down_kernel(meta_ref, h_ref, wd_ref, o_ref, wd_s, *, n_tiles):
    i = pl.program_id(1)

    @pl.when(_expert_changed(meta_ref, i))
    def _():
        wd_s[...] = wd_ref[...].astype(BF16)

    @pl.when(i < meta_ref[n_tiles])
    def _():
        o_ref[...] = _pack_bf16_pairs(jnp.dot(h_ref[...], wd_s[...], preferred_element_type=F32))

    @pl.when(i >= meta_ref[n_tiles])
    def _():
        o_ref[...] = jnp.zeros_like(o_ref)


def _expert_down_call(meta, hmid, w_down, l, n_tiles):
    FF = hmid.shape[1]
    D = w_down.shape[-1]
    tm = MOE_TILE
    return pl.pallas_call(
        functools.partial(_expert_down_kernel, n_tiles=n_tiles),
        out_shape=jax.ShapeDtypeStruct((n_tiles * tm, D // 2), jnp.uint32),
        grid_spec=pltpu.PrefetchScalarGridSpec(
            num_scalar_prefetch=1,
            grid=(1, n_tiles),
            in_specs=[pl.BlockSpec((tm, FF), lambda n, i, meta: (_last_used(i, meta, n_tiles), 0)),
                      pl.BlockSpec((None, None, FF, D), lambda n, i, meta: (l, meta[i], 0, 0))],
            out_specs=pl.BlockSpec((tm, D // 2), lambda n, i, meta: (i, 0)),
            scratch_shapes=[pltpu.VMEM((FF, D), BF16)]),
        compiler_params=_cparams(("arbitrary", "arbitrary"), VMEM_LIMIT),
        name="moe_expert_down",
    )(meta, hmid, w_down)


def _combine_kernel(pos_ref, ys_hbm, x_ref, w_ref, mod_ref, o_ref, buf, sem, *, tm, g_idx, n_tok):
    i = pl.program_id(0)

    def issue(tile, slot):
        base = tile * tm

        def body(g, carry):
            for u in range(DMA_UNROLL):
                r = g * DMA_UNROLL + u
                _row_copy(ys_hbm, buf.at[slot, 0], sem.at[slot, 0], pos_ref[base + r], r).start(priority=0)
                _row_copy(ys_hbm, buf.at[slot, 1], sem.at[slot, 1], pos_ref[n_tok + base + r], r).start(priority=1)
            return carry

        lax.fori_loop(0, tm // DMA_UNROLL, body, 0)

    @pl.when(i == 0)
    def _():
        issue(0, 0)

    @pl.when(i + 1 < n_tok // tm)
    def _():
        issue(i + 1, (i + 1) % 2)

    slot = i % 2
    for k in range(2):
        pltpu.make_async_copy(ys_hbm.at[pl.ds(0, tm)], buf.at[slot, k], sem.at[slot, k]).wait()
    w = w_ref[...]
    a1, a2 = _unpack_bf16_pairs(buf[slot, 0])
    b1, b2 = _unpack_bf16_pairs(buf[slot, 1])
    k = a1.shape[1]
    g = mod_ref[g_idx:g_idx + 1, :]
    o_ref[:, :k] = x_ref[:, :k] + g[:, :k] * (w[:, 0:1] * a1 + w[:, 1:2] * b1)
    o_ref[:, k:] = x_ref[:, k:] + g[:, k:] * (w[:, 0:1] * a2 + w[:, 1:2] * b2)


def _combine_call(pos2, ys, x, w_cols, mod, g_idx, n_prompt, t_s):
    N, D = x.shape
    tm = MOE_TILE
    row = _mod_row_map(tm, n_prompt, t_s)
    return pl.pallas_call(
        functools.partial(_combine_kernel, tm=tm, g_idx=g_idx, n_tok=N),
        out_shape=jax.ShapeDtypeStruct((N, D), F32),
        grid_spec=pltpu.PrefetchScalarGridSpec(
            num_scalar_prefetch=1,
            grid=(N // tm,),
            in_specs=[pl.BlockSpec(memory_space=pl.ANY),
                      pl.BlockSpec((tm, D), lambda i, pos: (i, 0)),
                      pl.BlockSpec((tm, LANES), lambda i, pos: (i, 0)),
                      pl.BlockSpec((None, 6, D), lambda i, pos: (row(i), 0, 0))],
            out_specs=pl.BlockSpec((tm, D), lambda i, pos: (i, 0)),
            scratch_shapes=[pltpu.VMEM((2, 2, tm, D // 2), jnp.uint32), pltpu.SemaphoreType.DMA((2, 2))]),
        compiler_params=_cparams(("arbitrary",), VMEM_LIMIT),
        name="moe_combine",
    )(pos2, ys, x, w_cols, mod)


def _moe_plan(idx, n_tiles):
    N = idx.shape[1]
    tm = MOE_TILE
    blk = 128
    e_flat = idx.reshape(-1)
    experts = jnp.arange(N_EXPERTS, dtype=jnp.int32)
    onehot = (e_flat[:, None] == experts[None, :]).astype(F32)
    oh3 = onehot.reshape(-1, blk, N_EXPERTS)
    tri = (jnp.arange(blk)[:, None] > jnp.arange(blk)[None, :]).astype(F32)
    within = jnp.einsum("ij,bjk->bik", tri, oh3, precision=lax.Precision.HIGHEST)
    blk_tot = jnp.sum(oh3, axis=1)
    before = jnp.cumsum(blk_tot, axis=0) - blk_tot
    rank = jnp.sum((within + before[:, None, :]) * oh3, axis=-1).reshape(-1).astype(jnp.int32)
    counts = jnp.sum(blk_tot, axis=0).astype(jnp.int32)
    tiles = (counts + tm - 1) // tm
    tile_end = jnp.cumsum(tiles)
    tile_start = tile_end - tiles
    start_of = jnp.sum(onehot.astype(jnp.int32) * tile_start[None, :], axis=1)
    pos = (start_of * tm + rank).astype(jnp.int32)
    tok = jnp.tile(jnp.arange(N, dtype=jnp.int32), 2)
    row_tok = jnp.zeros((n_tiles * tm,), jnp.int32).at[pos].set(tok)
    n_used = tile_end[-1]
    t = jnp.arange(n_tiles, dtype=jnp.int32)
    tile_exp = jnp.sum((tile_end[None, :] <= t[:, None]).astype(jnp.int32), axis=1)
    e_last = jnp.max(jnp.where(tiles > 0, experts, 0))
    tile_exp = jnp.where(t < n_used, tile_exp, e_last)
    meta = jnp.concatenate([tile_exp, n_used[None]]).astype(jnp.int32)
    return row_tok, meta, pos


def _moe(h2f, h2p, x, mod, router_wt, router_bias2, w_gate, w_up, w_down, l, n_prompt, t_s):
    N = x.shape[0]
    idx8, wgt8 = _router_call(h2f, router_wt, router_bias2)
    n_tiles = (2 * N) // MOE_TILE + N_EXPERTS
    row_tok, meta, pos2 = _moe_plan(idx8[:2], n_tiles)
    xs = _gather_call(row_tok, meta, h2p, n_tiles)
    hmid = _expert_up_call(meta, xs, w_gate, w_up, l, n_tiles)
    ys = _expert_down_call(meta, hmid, w_down, l, n_tiles)
    w_cols = jnp.zeros((N, LANES), F32).at[:, 0].set(wgt8[0]).at[:, 1].set(wgt8[1])
    return _combine_call(pos2, ys, x, w_cols, mod, 5, n_prompt, t_s)


def _rope_tables(T):
    nf = RET_D // 4
    t = jnp.arange(T)
    pos = jnp.stack([t // GRID_W, t % GRID_W], -1).astype(F32)
    inv = ROPE_BASE ** (-jnp.arange(nf, dtype=F32) / nf)
    ang = pos[:, :, None] * inv
    cos, sin = jnp.cos(ang), jnp.sin(ang)
    cos_t = jnp.concatenate([cos, cos], axis=-1).reshape(T, RET_D)
    sin_t = jnp.concatenate([-sin, sin], axis=-1).reshape(T, RET_D)
    return cos_t, sin_t


def kernel(x_prompt, x_sample, state_ret_fwd, state_ret_bwd, cache_na_k, cache_na_v, state_rwkv_fwd, state_rwkv_bwd, c, c_ctx, w_mod, b_mod, norm_mix, norm_ffn, w_in, w_gate, b_gate, w_br_ret, w_br_na, w_br_rw, w_out, ret_log_decay, ret_gn, na_qn, na_kn, na_rpb, rw_shift, rw_w0, rw_w1, rw_w2, rw_a0, rw_a1, rw_a2, rw_g1, rw_g2, rw_kk, rw_ka, rw_rk, rw_gn, router_w, router_bias, moe_w_gate, moe_w_up, moe_w_down):
    BP, TP, D = x_prompt.shape
    BS, TS, _ = x_sample.shape
    depth = w_in.shape[0]
    NP, NS = BP * TP, BS * TS
    N = NP + NS
    assert NP % (2 * ROW_TILE) == 0 and TS % (2 * ROW_TILE) == 0 and TP % RW_CHUNK == 0 and TS % RW_CHUNK == 0
    assert TS % TP == 0 and NP % TS == 0 and BS <= 7

    x = jnp.concatenate([x_prompt.reshape(NP, D), x_sample.reshape(NS, D)], axis=0)
    c8 = jnp.concatenate([c_ctx[None, :], c, jnp.zeros((7 - BS, D), F32)], axis=0)
    b_mod3 = b_mod.reshape(depth, 1, 6 * D)
    norm_mix3 = norm_mix.reshape(depth, 1, D)
    norm_ffn3 = norm_ffn.reshape(depth, 1, D)
    b_gate3 = b_gate.reshape(depth, 1, 3 * D)
    ret_gn3 = ret_gn.reshape(depth, 1, RET_HEADS * RET_D)
    na_qn3 = na_qn.reshape(depth, 1, NA_DH)
    na_kn3 = na_kn.reshape(depth, 1, NA_DH)
    rw_kk3 = rw_kk.reshape(depth, 1, RW_W)
    rw_ka3 = rw_ka.reshape(depth, 1, RW_W)
    rw_rk3 = rw_rk.reshape(depth, 1, RW_W)
    rw_gn3 = rw_gn.reshape(depth, 1, RW_W)
    zpad = jnp.zeros((depth, D, LANES - 64), F32)
    w_lora = jnp.concatenate([rw_w1[:, 0], zpad, rw_w1[:, 1], zpad, rw_a1[:, 0], zpad, rw_a1[:, 1], zpad,
                              rw_g1], axis=-1)
    kpad = jnp.zeros((depth, 2, LANES - 64, RW_W), F32)
    w2p = jnp.concatenate([rw_w2, kpad], axis=2)
    a2p = jnp.concatenate([rw_a2, kpad], axis=2)
    router_wt = router_w.T
    router_bias2 = router_bias.reshape(N_EXPERTS, 1)
    cos_t, sin_t = _rope_tables(TS)
    zero_st = jnp.zeros((BP, RW_HEADS // 2, LANES, LANES), F32)

    ret_f = jnp.zeros((BP, depth, RET_HEADS, RET_D, RET_D), F32)
    ret_b = jnp.zeros((BP, depth, RET_HEADS, RET_D, RET_D), F32)
    na_k = jnp.zeros((BP, depth, NA_HEADS, TP, NA_DH), F32)
    na_v = jnp.zeros((BP, depth, NA_HEADS, TP, NA_DH), F32)
    rw_f, rw_b = [], []
    for l in range(depth):
        mod = _mod_call(c8, w_mod, b_mod3, l)[:1 + BS].reshape(1 + BS, 6, D)
        (h,) = _norm_mod_call(x, norm_mix3, mod, l, 0, 1, NP, TS, (BF16,))
        proj = _mm_call(h, w_in, l, 2 * ROW_TILE, 512, "in_proj")
        hl = _mm_call(h, w_lora, l, ROW_TILE, LORA_W, "lora_proj")

        y_zero = jnp.zeros((N, RW_W), BF16)
        y_ret, ret_f, ret_b = _ret_call(proj, ret_log_decay[l], ret_gn3, l, BP, TP, 0, False,
                                        (y_zero, ret_f, ret_b))
        y_ret = _ret_call(proj, ret_log_decay[l], ret_gn3, l, BS, TS, NP, True,
                          (cos_t, sin_t, state_ret_fwd, state_ret_bwd, y_ret))
        y_na, na_k, na_v = _na_ctx_call(proj, na_qn3, na_kn3, y_zero, na_k, na_v, l, BP, TP)
        bias = _na_bias(na_rpb[l], TS // GRID_W)
        y_na = _na_lat_call(proj, cache_na_k, cache_na_v, bias, na_qn3, na_kn3, y_na, l, BS, TS, NP)
        y_rw = y_zero
        for (B, T, row0, s0f, s0b) in ((BP, TP, 0, zero_st, zero_st),
                                       (BS, TS, NP, _blockdiag_states(state_rwkv_fwd[:, l]),
                                        _blockdiag_states(state_rwkv_bwd[:, l]))):
            (r, v, al, g, bonus, lw0, be0, kd0, lw1, be1, kd1) = _rw_prep_call(
                proj, hl, rw_shift, w2p, a2p, rw_g2, rw_w0, rw_a0, rw_kk3, rw_ka3, rw_rk3, l, B, T, row0)
            yf, stf = _rw_scan_call(r, v, al, lw0, be0, kd0, s0f, B, T, False)
            yb, stb = _rw_scan_call(r, v, al, lw1, be1, kd1, s0b, B, T, True)
            y_rw = _rw_post_call(yf, yb, bonus, g, rw_gn3, y_rw, l, row0)
            if row0 == 0:
                rw_f.append(_unblock_states(stf))
                rw_b.append(_unblock_states(stb))

        merged = _merge_call(h, y_ret, y_na, y_rw, w_gate, b_gate3, w_br_ret, w_br_na, w_br_rw, l)
        x = _out_call(merged, w_out, x, mod, l, 2, NP, TS)
        h2f, h2p = _norm_mod_call(x, norm_ffn3, mod, l, 3, 4, NP, TS, (F32, jnp.uint32))
        x = _moe(h2f, h2p, x, mod, router_wt, router_bias2, moe_w_gate, moe_w_up, moe_w_down, l, NP, TS)

    y_p = x[:NP].reshape(BP, TP, D)
    y_s = x[NP:].reshape(BS, TS, D)
    return (y_p, y_s, ret_f, ret_b, na_k, na_v, jnp.stack(rw_f, axis=1), jnp.stack(rw_b, axis=1))
```
